```python
import jax, jax.numpy as jnp
from jax import lax
import numpy as np

D_MODEL = 1024
BATCH = 8
SEQ = 4096
DEPTH = 4

CHUNK = 64
PLE_DIM = 256
N_EVEN = (DEPTH + 1) // 2
N_ODD = DEPTH // 2
D_A = D_MODEL
CONV_A = 31
D_B = D_MODEL
HEAD_DIM = 64
H_B = D_B // HEAD_DIM
N_GROUPS = 4
N_STATE = 128
CONV_B = 4
XBC_DIM = D_B + 2 * N_GROUPS * N_STATE
E_IN = 2 * D_A + D_B + XBC_DIM + H_B
D_C = D_MODEL
CONV_C = 3
D_FF = 2816
CONV_F = 3
LN_EPS = 1e-5

kernel_name = "hybrid_conformer_ssd_shortconv_trunk"


def _layer_norm(x, g, b):
    xf = x.astype(jnp.float32)
    mu = jnp.mean(xf, axis=-1, keepdims=True)
    var = jnp.mean(jnp.square(xf - mu), axis=-1, keepdims=True)
    return ((xf - mu) * lax.rsqrt(var + LN_EPS) * g + b).astype(x.dtype)


def _rms_norm(x, g):
    xf = x.astype(jnp.float32)
    return (xf * lax.rsqrt(jnp.mean(jnp.square(xf), axis=-1, keepdims=True) + LN_EPS) * g).astype(x.dtype)


def _dwconv_causal(x, w, b=None):
    k, c = w.shape
    y = lax.conv_general_dilated(x, w[:, None, :].astype(x.dtype), window_strides=(1,),
                                 padding=[(k - 1, 0)], dimension_numbers=("NWC", "WIO", "NWC"),
                                 feature_group_count=c)
    if b is not None:
        y = y + b
    return y


def _conformer_conv(u, conv_w, conv_b, ln_g, ln_b):
    a = u[..., :D_A] * jax.nn.sigmoid(u[..., D_A:])
    a = _dwconv_causal(a, conv_w, conv_b)
    return jax.nn.silu(_layer_norm(a, ln_g, ln_b))


def _ssd(xh, dt, a, bm, cm):
    bsz, l, h, p = xh.shape
    g, n = bm.shape[2], bm.shape[3]
    hg = h // g
    c = l // CHUNK
    x = xh.reshape(bsz, c, CHUNK, g, hg, p)
    dtc = dt.reshape(bsz, c, CHUNK, g, hg)
    bc = bm.reshape(bsz, c, CHUNK, g, n)
    cc = cm.reshape(bsz, c, CHUNK, g, n)
    cum = jnp.cumsum(dtc * a.reshape(g, hg), axis=2)
    mask = jnp.tril(jnp.ones((CHUNK, CHUNK), dtype=bool))[:, :, None, None]
    seg = cum[:, :, :, None] - cum[:, :, None, :]
    decay = jnp.where(mask, jnp.exp(jnp.where(mask, seg, 0.0)), 0.0)
    cb = jnp.einsum("bclgn,bcsgn->bclsg", cc, bc)
    w = cb[..., None] * decay * dtc[:, :, None]
    y_diag = jnp.einsum("bclsgh,bcsghp->bclghp", w, x)
    decay_states = jnp.exp(cum[:, :, -1:] - cum) * dtc
    states = jnp.einsum("bclgn,bclgh,bclghp->bcghpn", bc, decay_states, x)
    chunk_decay = jnp.exp(cum[:, :, -1])

    def step(hstate, inp):
        dec, st = inp
        return hstate * dec[..., None, None] + st, hstate

    h0 = jnp.zeros((bsz, g, hg, p, n), dtype=states.dtype)
    _, prev = lax.scan(step, h0, (jnp.moveaxis(chunk_decay, 1, 0), jnp.moveaxis(states, 1, 0)))
    prev = jnp.moveaxis(prev, 0, 1)
    y_off = jnp.einsum("bclgn,bcghpn,bclgh->bclghp", cc, prev, jnp.exp(cum))
    return (y_diag + y_off).reshape(bsz, l, h, p).astype(xh.dtype)


def _mamba2(z, xbc, dt_raw, conv_w, conv_b, dt_bias, a_log, d_skip, norm_g):
    bsz, l, _ = z.shape
    xbc = jax.nn.silu(_dwconv_causal(xbc, conv_w, conv_b))
    gn = N_GROUPS * N_STATE
    xs = xbc[..., :D_B].reshape(bsz, l, H_B, HEAD_DIM)
    bm = xbc[..., D_B:D_B + gn].reshape(bsz, l, N_GROUPS, N_STATE)
    cm = xbc[..., D_B + gn:].reshape(bsz, l, N_GROUPS, N_STATE)
    dt = jax.nn.softplus(dt_raw.astype(jnp.float32) + dt_bias)
    a = -jnp.exp(a_log.astype(jnp.float32))
    y = _ssd(xs, dt, a, bm, cm) + d_skip[:, None] * xs
    y = y.reshape(bsz, l, D_B) * jax.nn.silu(z)
    return _rms_norm(y, norm_g)


def _short_conv(x, w_in, conv_w, w_out):
    u = x @ w_in
    bg, cg, v = u[..., :D_C], u[..., D_C:2 * D_C], u[..., 2 * D_C:]
    return (bg * _dwconv_causal(cg * v, conv_w)) @ w_out


def _conv_ffn(x, w_up, conv_w, conv_b, w_down):
    h = _dwconv_causal(x @ w_up, conv_w, conv_b)
    return (jax.nn.silu(h[..., :D_FF]) * h[..., D_FF:]) @ w_down


def setup_inputs(seed: int = 0) -> dict:
    key = jax.random.key(seed)
    ks = jax.random.split(key, 32)
    beta = (8.0 * DEPTH) ** -0.25
    nrm = jax.random.normal

    def dense(k, shape, fan_in, scale=1.0):
        return nrm(k, shape, jnp.float32) * (fan_in ** -0.5) * scale

    dt0 = jnp.exp(jax.random.uniform(ks[8], (N_EVEN, H_B), jnp.float32) * (np.log(0.1) - np.log(0.001)) + np.log(0.001))
    return {
        "x": nrm(ks[0], (BATCH, SEQ, D_MODEL), jnp.float32),
        "p": nrm(ks[1], (DEPTH, BATCH, SEQ, PLE_DIM), jnp.float32),
        "e_w_in": dense(ks[2], (N_EVEN, D_MODEL, E_IN), D_MODEL),
        "e_conv_a_w": dense(ks[3], (N_EVEN, CONV_A, D_A), CONV_A),
        "e_conv_a_b": 0.02 * nrm(ks[4], (N_EVEN, D_A), jnp.float32),
        "e_ln_a_g": 1.0 + 0.02 * nrm(ks[5], (N_EVEN, D_A), jnp.float32),
        "e_ln_a_b": 0.02 * nrm(ks[6], (N_EVEN, D_A), jnp.float32),
        "e_conv_b_w": dense(ks[7], (N_EVEN, CONV_B, XBC_DIM), CONV_B),
        "e_conv_b_b": 0.02 * nrm(ks[9], (N_EVEN, XBC_DIM), jnp.float32),
        "e_dt_bias": dt0 + jnp.log(-jnp.expm1(-dt0)),
        "e_a_log": jnp.log(jax.random.uniform(ks[10], (N_EVEN, H_B), jnp.float32, 1.0, 16.0)),
        "e_d_skip": 1.0 + 0.1 * nrm(ks[11], (N_EVEN, H_B), jnp.float32),
        "e_norm_b_g": 1.0 + 0.02 * nrm(ks[12], (N_EVEN, D_B), jnp.float32),
        "e_w_out": dense(ks[13], (N_EVEN, D_A + D_B, D_MODEL), D_A + D_B, beta),
        "o_w_in": dense(ks[14], (N_ODD, D_MODEL, 3 * D_C), D_MODEL),
        "o_conv_w": dense(ks[15], (N_ODD, CONV_C, D_C), CONV_C),
        "o_w_out": dense(ks[16], (N_ODD, D_C, D_MODEL), D_C, beta),
        "f_w_up": dense(ks[17], (DEPTH, D_MODEL, 2 * D_FF), D_MODEL),
        "f_conv_w": dense(ks[18], (DEPTH, CONV_F, 2 * D_FF), CONV_F),
        "f_conv_b": 0.02 * nrm(ks[19], (DEPTH, 2 * D_FF), jnp.float32),
        "f_w_down": dense(ks[20], (DEPTH, D_FF, D_MODEL), D_FF, beta),
        "ple_w_proj": dense(ks[21], (DEPTH, PLE_DIM, D_MODEL), PLE_DIM, beta),
        "ple_w_gate": dense(ks[22], (DEPTH, D_MODEL, D_MODEL), D_MODEL),
        "ln_g": 1.0 + 0.02 * nrm(ks[23], (DEPTH, 2, D_MODEL), jnp.float32),
        "ln_b": 0.02 * nrm(ks[24], (DEPTH, 2, D_MODEL), jnp.float32),
    }


def reference(x, p, e_w_in, e_conv_a_w, e_conv_a_b, e_ln_a_g, e_ln_a_b, e_conv_b_w, e_conv_b_b,
              e_dt_bias, e_a_log, e_d_skip, e_norm_b_g, e_w_out, o_w_in, o_conv_w, o_w_out,
              f_w_up, f_conv_w, f_conv_b, f_w_down, ple_w_proj, ple_w_gate, ln_g, ln_b):
    alpha = (2.0 * DEPTH) ** 0.25
    o_a = 2 * D_A
    o_x = o_a + D_B
    o_dt = o_x + XBC_DIM
    for i in range(DEPTH):
        j = i // 2
        if i % 2 == 0:
            u = x @ e_w_in[j]
            ya = _conformer_conv(u[..., :o_a], e_conv_a_w[j], e_conv_a_b[j], e_ln_a_g[j], e_ln_a_b[j])
            yb = _mamba2(u[..., o_a:o_x], u[..., o_x:o_dt], u[..., o_dt:], e_conv_b_w[j], e_conv_b_b[j],
                         e_dt_bias[j], e_a_log[j], e_d_skip[j], e_norm_b_g[j])
            mix = jnp.concatenate([ya, yb], axis=-1) @ e_w_out[j]
        else:
            mix = _short_conv(x, o_w_in[j], o_conv_w[j], o_w_out[j])
        x = _layer_norm(alpha * x + mix, ln_g[i, 0], ln_b[i, 0])
        ffn = _conv_ffn(x, f_w_up[i], f_conv_w[i], f_conv_b[i], f_w_down[i])
        ple = (p[i] @ ple_w_proj[i]) * jax.nn.sigmoid(x @ ple_w_gate[i])
        x = _layer_norm(alpha * x + ffn + ple, ln_g[i, 1], ln_b[i, 1])
    return x
```

```python
import functools

import jax
import jax.numpy as jnp
from jax import lax
from jax.experimental import pallas as pl
from jax.experimental.pallas import tpu as pltpu

F32 = jnp.float32
BF16 = jnp.bfloat16

LN_EPS = 1e-5
HEAD_DIM = 64
N_GROUPS = 4
N_STATE = 128
LANES = 128
SUBLANES = 8
SSD_Q = 128
VMEM_LIMIT = 56 * 1024 * 1024


def _dot(a, b):
    return jnp.dot(a, b, preferred_element_type=F32)


def _layer_norm(v, g, b):
    mu = jnp.mean(v, axis=-1, keepdims=True)
    d = v - mu
    var = jnp.mean(d * d, axis=-1, keepdims=True)
    return d * lax.rsqrt(var + LN_EPS) * g + b


def _sigmoid(v):
    return 1.0 / (1.0 + jnp.exp(-v))


def _silu(v):
    return v * _sigmoid(v)


def _softplus(v):
    return jnp.maximum(v, 0.0) + jnp.log(1.0 + jnp.exp(-jnp.abs(v)))


def _causal_conv(h, stage_ref, carry_ref, cols, w, hist):
    tl = h.shape[0]
    k = w.shape[0]
    stage_ref[0:hist, :] = carry_ref[:, cols]
    stage_ref[hist:hist + tl, :] = h
    carry_ref[:, cols] = h[tl - hist:, :]
    out = h * w[k - 1:k, :]
    for j in range(k - 1):
        shift = k - 1 - j
        out = out + stage_ref[hist - shift:hist - shift + tl, :] * w[j:j + 1, :]
    return out


def _ffn_kernel(x_ref, p_ref, wup_ref, cw_ref, cb_ref, wdn_ref, wproj_ref, wgate_ref,
                g_ref, b_ref, o_ref, stage_ref, carry_ref, act_ref, *, d_ff, ck, alpha):
    @pl.when(pl.program_id(1) == 0)
    def _():
        carry_ref[...] = jnp.zeros_like(carry_ref)

    x = x_ref[0]
    xb = x.astype(BF16)
    hist = carry_ref.shape[0]

    def conv_cols(col0, slot):
        cols = slice(col0, col0 + ck)
        h = _dot(xb, wup_ref[:, cols])
        out = _causal_conv(h, stage_ref.at[slot], carry_ref, cols, cw_ref[:, cols], hist)
        return out + cb_ref[:, cols]

    for c in range(d_ff // ck):
        ha = conv_cols(c * ck, 0)
        hg = conv_cols(d_ff + c * ck, 1)
        act_ref[:, c * ck:(c + 1) * ck] = (_silu(ha) * hg).astype(BF16)

    ffn = _dot(act_ref[...], wdn_ref[...])
    gate = _sigmoid(_dot(xb, wgate_ref[...]))
    ple = _dot(p_ref[0, 0].astype(BF16), wproj_ref[...]) * gate
    o_ref[0] = _layer_norm(alpha * x + ffn + ple, g_ref[...], b_ref[...])


def _const_spec(shape):
    nd = len(shape)
    return pl.BlockSpec(shape, lambda b, t: (0,) * nd, pipeline_mode=pl.Buffered(1))


def _ffn_call(x, p, layer, w_up, conv_w, conv_b, w_down, w_proj, w_gate, ln_g, ln_b, *, alpha, tl):
    bsz, seq, d = x.shape
    d_ff = w_down.shape[0]
    ple_dim = p.shape[-1]
    ck = 256
    hist = SUBLANES
    kern = functools.partial(_ffn_kernel, d_ff=d_ff, ck=ck, alpha=alpha)
    return pl.pallas_call(
        kern,
        grid=(bsz, seq // tl),
        in_specs=[
            pl.BlockSpec((1, tl, d), lambda b, t: (b, t, 0)),
            pl.BlockSpec((1, 1, tl, ple_dim), lambda b, t: (layer, b, t, 0)),
            _const_spec(w_up.shape), _const_spec(conv_w.shape), _const_spec(conv_b.shape),
            _const_spec(w_down.shape), _const_spec(w_proj.shape), _const_spec(w_gate.shape),
            _const_spec(ln_g.shape), _const_spec(ln_b.shape),
        ],
        out_specs=pl.BlockSpec((1, tl, d), lambda b, t: (b, t, 0)),
        out_shape=jax.ShapeDtypeStruct(x.shape, F32),
        scratch_shapes=[
            pltpu.VMEM((2, hist + tl, ck), F32),
            pltpu.VMEM((hist, 2 * d_ff), F32),
            pltpu.VMEM((tl, d_ff), BF16),
        ],
        compiler_params=pltpu.CompilerParams(
            dimension_semantics=("arbitrary", "arbitrary"), vmem_limit_bytes=VMEM_LIMIT),
        name="conv_ffn",
    )(x, p, w_up, conv_w, conv_b, w_down, w_proj, w_gate, ln_g, ln_b)


def _odd_kernel(x_ref, win_ref, cw_ref, wout_ref, g_ref, b_ref, o_ref,
                stage_ref, carry_ref, gated_ref, *, d_c, ck, alpha):
    @pl.when(pl.program_id(1) == 0)
    def _():
        carry_ref[...] = jnp.zeros_like(carry_ref)

    x = x_ref[0]
    xb = x.astype(BF16)
    hist = carry_ref.shape[0]
    for c in range(d_c // ck):
        cols = slice(c * ck, (c + 1) * ck)
        bg = _dot(xb, win_ref[:, c * ck:(c + 1) * ck])
        cg = _dot(xb, win_ref[:, d_c + c * ck:d_c + (c + 1) * ck])
        v = _dot(xb, win_ref[:, 2 * d_c + c * ck:2 * d_c + (c + 1) * ck])
        conv = _causal_conv(cg * v, stage_ref, carry_ref, cols, cw_ref[:, cols], hist)
        gated_ref[:, cols] = (bg * conv).astype(BF16)
    mix = _dot(gated_ref[...], wout_ref[...])
    o_ref[0] = _layer_norm(alpha * x + mix, g_ref[...], b_ref[...])


def _odd_call(x, w_in, conv_w, w_out, ln_g, ln_b, *, alpha, tl):
    bsz, seq, d = x.shape
    d_c = w_out.shape[0]
    ck = 256
    hist = SUBLANES
    kern = functools.partial(_odd_kernel, d_c=d_c, ck=ck, alpha=alpha)
    return pl.pallas_call(
        kern,
        grid=(bsz, seq // tl),
        in_specs=[
            pl.BlockSpec((1, tl, d), lambda b, t: (b, t, 0)),
            _const_spec(w_in.shape), _const_spec(conv_w.shape), _const_spec(w_out.shape),
            _const_spec(ln_g.shape), _const_spec(ln_b.shape),
        ],
        out_specs=pl.BlockSpec((1, tl, d), lambda b, t: (b, t, 0)),
        out_shape=jax.ShapeDtypeStruct(x.shape, F32),
        scratch_shapes=[
            pltpu.VMEM((hist + tl, ck), F32),
            pltpu.VMEM((hist, d_c), F32),
            pltpu.VMEM((tl, d_c), BF16),
        ],
        compiler_params=pltpu.CompilerParams(
            dimension_semantics=("arbitrary", "arbitrary"), vmem_limit_bytes=VMEM_LIMIT),
        name="short_conv_mixer",
    )(x, w_in, conv_w, w_out, ln_g, ln_b)


def _split3(v):
    hi = v.astype(BF16)
    r1 = v - hi.astype(F32)
    mid = r1.astype(BF16)
    lo = (r1 - mid.astype(F32)).astype(BF16)
    return hi, mid, lo


def _even_kernel(x_ref, win_ref, wdt_ref, caw_ref, cab_ref, lag_ref, lab_ref, cbw_ref, cbb_ref,
                 dtb_ref, alog_ref, dskip_ref, nbg_ref, expand_ref, wout_ref, g_ref, b_ref,
                 o_ref,
                 astage_ref, acarry_ref, bstage_ref, bcarry_ref, ca_ref, xbc_ref, ssd_ref,
                 state_ref, cat_ref, *, d_a, d_b, alpha):
    @pl.when(pl.program_id(1) == 0)
    def _():
        acarry_ref[...] = jnp.zeros_like(acarry_ref)
        bcarry_ref[...] = jnp.zeros_like(bcarry_ref)
        state_ref[...] = jnp.zeros_like(state_ref)

    x = x_ref[0]
    tl = x.shape[0]
    xb = x.astype(BF16)
    ck = astage_ref.shape[1]
    ahist = acarry_ref.shape[0]
    bhist = bcarry_ref.shape[0]
    gn = N_GROUPS * N_STATE
    o_z = 2 * d_a
    o_x = o_z + d_b

    for c in range(d_a // ck):
        cols = slice(c * ck, (c + 1) * ck)
        lin = _dot(xb, win_ref[:, c * ck:(c + 1) * ck])
        gat = _dot(xb, win_ref[:, d_a + c * ck:d_a + (c + 1) * ck])
        a = lin * _sigmoid(gat)
        ca_ref[:, cols] = _causal_conv(a, astage_ref, acarry_ref, cols, caw_ref[:, cols], ahist) + cab_ref[:, cols]
    ya = _silu(_layer_norm(ca_ref[...], lag_ref[...], lab_ref[...]))
    cat_ref[:, 0:d_a] = ya.astype(BF16)

    for c in range((d_b + 2 * gn) // ck):
        cols = slice(c * ck, (c + 1) * ck)
        h = _dot(xb, win_ref[:, o_x + c * ck:o_x + (c + 1) * ck])
        conv = _causal_conv(h, bstage_ref, bcarry_ref, cols, cbw_ref[:, cols], bhist) + cbb_ref[:, cols]
        xbc_ref[:, cols] = _silu(conv)

    dt_all = _softplus(_dot(xb, wdt_ref[...]) + dtb_ref[...])
    da_all = dt_all * (-jnp.exp(alog_ref[...]))

    row = lax.broadcasted_iota(jnp.int32, (SSD_Q, SSD_Q), 0)
    col = lax.broadcasted_iota(jnp.int32, (SSD_Q, SSD_Q), 1)
    causal = row >= col
    tril = jnp.where(causal, 1.0, 0.0).astype(BF16)
    lane = lax.broadcasted_iota(jnp.int32, (SSD_Q, LANES), 1)
    half = [lane < HEAD_DIM, lane >= HEAD_DIM]
    heads_per_group = (d_b // HEAD_DIM) // N_GROUPS
    gw = heads_per_group * HEAD_DIM

    for q in range(tl // SSD_Q):
        rows = slice(q * SSD_Q, (q + 1) * SSD_Q)
        dt = dt_all[rows]
        hi, mid, lo = _split3(da_all[rows])
        cum = _dot(tril, hi) + _dot(tril, mid) + _dot(tril, lo)
        cum_t = cum.T
        dt_t = dt.T
        last = cum[SSD_Q - 1:SSD_Q, :]
        dstate = jnp.exp(last - cum) * dt
        pad = jnp.zeros((SUBLANES - 1, LANES), F32)
        scal = jnp.concatenate([dstate, jnp.exp(last), pad], axis=0)
        s_hi = scal.astype(BF16)
        s_lo = (scal - s_hi.astype(F32)).astype(BF16)
        expd = _dot(jnp.concatenate([s_hi, s_lo], axis=1), expand_ref[...])
        dstate_x = expd[0:SSD_Q]
        cdecay_x = expd[SSD_Q:SSD_Q + 1]

        xs = xbc_ref[rows, 0:d_b]
        xs_b = xs.astype(BF16)
        xs_scaled = (xs * dstate_x).astype(BF16)
        st_b = state_ref[...].astype(BF16)

        for g in range(N_GROUPS):
            bm = xbc_ref[rows, d_b + g * N_STATE:d_b + (g + 1) * N_STATE].astype(BF16)
            cm = xbc_ref[rows, d_b + gn + g * N_STATE:d_b + gn + (g + 1) * N_STATE]
            cb = lax.dot_general(cm.astype(BF16), bm, (((1,), (1,)), ((), ())),
                                 preferred_element_type=F32)
            for pr in range(heads_per_group // 2):
                lanes = slice(g * gw + pr * LANES, g * gw + (pr + 1) * LANES)
                xp = xs_b[:, lanes]
                sp = st_b[:, lanes]
                lhs, rhs = [], []
                for j in range(2):
                    hd = (g * gw + pr * LANES) // HEAD_DIM + j
                    ccol = jnp.broadcast_to(cum[:, hd:hd + 1], (SSD_Q, SSD_Q))
                    crow = jnp.broadcast_to(cum_t[hd:hd + 1, :], (SSD_Q, SSD_Q))
                    drow = jnp.broadcast_to(dt_t[hd:hd + 1, :], (SSD_Q, SSD_Q))
                    decay = jnp.where(causal, jnp.exp(jnp.where(causal, ccol - crow, 0.0)), 0.0)
                    lhs.append((cb * decay * drow).astype(BF16))
                    lhs.append((cm * jnp.exp(ccol)).astype(BF16))
                    rhs.append(jnp.where(half[j], xp, jnp.zeros_like(xp)))
                    rhs.append(jnp.where(half[j], sp, jnp.zeros_like(sp)))
                ssd_ref[rows, lanes] = _dot(jnp.concatenate(lhs, axis=1), jnp.concatenate(rhs, axis=0))
            gcols = slice(g * gw, (g + 1) * gw)
            upd = lax.dot_general(bm, xs_scaled[:, gcols], (((0,), (0,)), ((), ())),
                                  preferred_element_type=F32)
            state_ref[:, gcols] = state_ref[:, gcols] * cdecay_x[:, gcols] + upd

    xs_all = xbc_ref[:, 0:d_b]
    z = _dot(xb, win_ref[:, o_z:o_z + d_b])
    y = (ssd_ref[...] + dskip_ref[...] * xs_all) * _silu(z)
    yb = y * lax.rsqrt(jnp.mean(y * y, axis=-1, keepdims=True) + LN_EPS) * nbg_ref[...]
    cat_ref[:, d_a:d_a + d_b] = yb.astype(BF16)

    mix = _dot(cat_ref[...], wout_ref[...])
    o_ref[0] = _layer_norm(alpha * x + mix, g_ref[...], b_ref[...])


def _even_call(x, w_main, w_dt, conv_a_w, conv_a_b, ln_a_g, ln_a_b, conv_b_w, conv_b_b,
               dt_bias, a_log, d_skip_x, norm_b_g, expand, w_out, ln_g, ln_b, *, alpha, tl):
    bsz, seq, d = x.shape
    d_a = conv_a_w.shape[1]
    d_b = norm_b_g.shape[1]
    xbc_dim = conv_b_w.shape[1]
    ck = 256
    ahist = 32
    bhist = SUBLANES
    assert conv_a_w.shape[0] - 1 <= ahist and conv_b_w.shape[0] - 1 <= bhist
    kern = functools.partial(_even_kernel, d_a=d_a, d_b=d_b, alpha=alpha)
    consts = (w_main, w_dt, conv_a_w, conv_a_b, ln_a_g, ln_a_b, conv_b_w, conv_b_b,
              dt_bias, a_log, d_skip_x, norm_b_g, expand, w_out, ln_g, ln_b)
    return pl.pallas_call(
        kern,
        grid=(bsz, seq // tl),
        in_specs=[pl.BlockSpec((1, tl, d), lambda b, t: (b, t, 0))] + [_const_spec(c.shape) for c in consts],
        out_specs=pl.BlockSpec((1, tl, d), lambda b, t: (b, t, 0)),
        out_shape=jax.ShapeDtypeStruct(x.shape, F32),
        scratch_shapes=[
            pltpu.VMEM((ahist + tl, ck), F32),
            pltpu.VMEM((ahist, d_a), F32),
            pltpu.VMEM((bhist + tl, ck), F32),
            pltpu.VMEM((bhist, xbc_dim), F32),
            pltpu.VMEM((tl, d_a), F32),
            pltpu.VMEM((tl, xbc_dim), F32),
            pltpu.VMEM((tl, d_b), F32),
            pltpu.VMEM((N_STATE, d_b), F32),
            pltpu.VMEM((tl, d_a + d_b), BF16),
        ],
        compiler_params=pltpu.CompilerParams(
            dimension_semantics=("arbitrary", "arbitrary"), vmem_limit_bytes=VMEM_LIMIT),
        name="conformer_ssd_mixer",
    )(x, *consts)


def _row(v):
    return v.reshape(1, -1).astype(F32)


def _pad_lanes(v):
    return jnp.pad(v, ((0, 0), (0, LANES - v.shape[1])))


def kernel(x, p, e_w_in, e_conv_a_w, e_conv_a_b, e_ln_a_g, e_ln_a_b, e_conv_b_w, e_conv_b_b, e_dt_bias, e_a_log, e_d_skip, e_norm_b_g, e_w_out, o_w_in, o_conv_w, o_w_out, f_w_up, f_conv_w, f_conv_b, f_w_down, ple_w_proj, ple_w_gate, ln_g, ln_b):
    depth = f_w_up.shape[0]
    alpha = (2.0 * depth) ** 0.25
    d_b = e_norm_b_g.shape[1]
    n_heads = e_dt_bias.shape[1]
    o_dt = e_w_in.shape[2] - n_heads
    tl_even, tl_odd, tl_ffn = 256, 512, 512

    head_of_col = jnp.arange(d_b) // HEAD_DIM
    sel = (jnp.arange(LANES)[:, None] == head_of_col[None, :]).astype(BF16)
    expand = jnp.concatenate([sel, sel], axis=0)

    for i in range(depth):
        j = i // 2
        if i % 2 == 0:
            w_in = e_w_in[j]
            x = _even_call(
                x, w_in[:, :o_dt].astype(BF16), _pad_lanes(w_in[:, o_dt:]).astype(BF16),
                e_conv_a_w[j], _row(e_conv_a_b[j]), _row(e_ln_a_g[j]), _row(e_ln_a_b[j]),
                e_conv_b_w[j], _row(e_conv_b_b[j]),
                _pad_lanes(_row(e_dt_bias[j])), _pad_lanes(_row(e_a_log[j])),
                _row(jnp.repeat(e_d_skip[j], HEAD_DIM)), _row(e_norm_b_g[j]), expand,
                e_w_out[j].astype(BF16), _row(ln_g[i, 0]), _row(ln_b[i, 0]),
                alpha=alpha, tl=tl_even)
        else:
            x = _odd_call(x, o_w_in[j].astype(BF16), o_conv_w[j], o_w_out[j].astype(BF16),
                          _row(ln_g[i, 0]), _row(ln_b[i, 0]), alpha=alpha, tl=tl_odd)
        x = _ffn_call(x, p, i, f_w_up[i].astype(BF16), f_conv_w[i], _row(f_conv_b[i]),
                      f_w_down[i].astype(BF16), ple_w_proj[i].astype(BF16), ple_w_gate[i].astype(BF16),
                      _row(ln_g[i, 1]), _row(ln_b[i, 1]), alpha=alpha, tl=tl_ffn)
    return x
```

```python
import functools

import jax
import jax.numpy as jnp
from jax import lax
from jax.experimental import pallas as pl
from jax.experimental.pallas import tpu as pltpu

F32 = jnp.float32
BF16 = jnp.bfloat16

LN_EPS = 1e-5
HEAD_DIM = 64
N_GROUPS = 4
N_STATE = 128
LANES = 128
SUBLANES = 8
PERM_TILE = 128
S_GROUPS = PERM_TILE // SUBLANES
CONV_ROW_BLOCK = 8
VMEM_LIMIT = 56 * 1024 * 1024


def _dot(a, b):
    return jnp.dot(a, b, preferred_element_type=F32)


def _layer_norm(v, g, b):
    mu = jnp.mean(v, axis=-1, keepdims=True)
    d = v - mu
    var = jnp.mean(d * d, axis=-1, keepdims=True)
    return d * lax.rsqrt(var + LN_EPS) * g + b


def _sigmoid(v):
    return 1.0 / (1.0 + jnp.exp(-v))


def _silu(v):
    return v * _sigmoid(v)


def _softplus(v):
    return jnp.maximum(v, 0.0) + jnp.log(1.0 + jnp.exp(-jnp.abs(v)))


def _load_sublane_major(refs, lead, n_blocks):
    groups = []
    for u in range(n_blocks):
        for i in range(S_GROUPS):
            rows = pl.ds(u * PERM_TILE + i, SUBLANES, stride=S_GROUPS)
            groups.append(jnp.concatenate([r[lead + (rows, slice(None))] for r in refs], axis=1))
    return jnp.concatenate(groups, axis=0)


def _stage_history(h3, stage_ref, carry, hist):
    n1 = min(hist, S_GROUPS)
    n2 = hist - n1
    assert n2 <= S_GROUPS
    rolled = pltpu.roll(h3[S_GROUPS - n1:], 1, axis=1)
    if n2:
        rolled = jnp.concatenate([pltpu.roll(h3[S_GROUPS - n2:], 2, axis=1), rolled], axis=0)
    sub = lax.broadcasted_iota(jnp.int32, rolled.shape, 1)
    grp = lax.broadcasted_iota(jnp.int32, rolled.shape, 0)
    from_prev = sub < jnp.where(grp < n2, 2, 1)
    stage_ref[0:hist] = jnp.where(from_prev, carry, rolled)
    stage_ref[hist:hist + S_GROUPS] = h3
    return rolled


def _short_conv(h, stage_ref, carry_ref, cols, w):
    tl, n = h.shape
    k = w.shape[0]
    hist = k - 1
    h3 = h.reshape(tl // SUBLANES, SUBLANES, n)
    carry = carry_ref[:, :, cols]
    outs = []
    for u in range(tl // PERM_TILE):
        hu = h3[u * S_GROUPS:(u + 1) * S_GROUPS]
        carry = _stage_history(hu, stage_ref.at[u], carry, hist)
        out = hu * w[k - 1:k, :]
        for j in range(k - 1):
            out = out + stage_ref[u, j:j + S_GROUPS] * w[j:j + 1, :]
        outs.append(out)
    carry_ref[:, :, cols] = carry
    return jnp.concatenate(outs, axis=0).reshape(tl, n)


def _ffn_kernel(x_ref, p0_ref, p1_ref, wup_ref, cw_ref, cb_ref, wdn_ref, wproj_ref, wgate_ref,
                g_ref, b_ref, o_ref, stage_ref, carry_ref, act_ref, *unperm, d_ff, ck, alpha):
    @pl.when(pl.program_id(1) == 0)
    def _():
        carry_ref[...] = jnp.zeros_like(carry_ref)

    x = x_ref[0]
    tl, d = x.shape
    xb = x.astype(BF16)

    def conv_cols(col0, slot):
        cols = slice(col0, col0 + ck)
        h = _dot(xb, wup_ref[:, cols])
        return _short_conv(h, stage_ref.at[slot], carry_ref, cols, cw_ref[:, cols]) + cb_ref[:, cols]

    for c in range(d_ff // ck):
        ha = conv_cols(c * ck, 0)
        hg = conv_cols(d_ff + c * ck, 1)
        act_ref[:, c * ck:(c + 1) * ck] = (_silu(ha) * hg).astype(BF16)

    ffn = _dot(act_ref[...], wdn_ref[...])
    gate = _sigmoid(_dot(xb, wgate_ref[...]))
    pe = _load_sublane_major((p0_ref, p1_ref), (0, 0), tl // PERM_TILE)
    ple = _dot(pe.astype(BF16), wproj_ref[...]) * gate
    y = _layer_norm(alpha * x + ffn + ple, g_ref[...], b_ref[...])
    if not unperm:
        o_ref[0] = y
    else:
        nat_ref, = unperm
        y3 = y.reshape(tl // SUBLANES, SUBLANES, d)
        for u in range(tl // PERM_TILE):
            for i in range(S_GROUPS):
                rows = pl.ds(u * PERM_TILE + i, SUBLANES, stride=S_GROUPS)
                for kb in range(d // LANES):
                    nat_ref[kb, rows, :] = y3[u * S_GROUPS + i, :, kb * LANES:(kb + 1) * LANES]
        for kb in range(d // LANES):
            o_ref[0, :, kb * LANES:(kb + 1) * LANES] = nat_ref[kb]


def _const_spec(shape):
    nd = len(shape)
    return pl.BlockSpec(shape, lambda b, t: (0,) * nd, pipeline_mode=pl.Buffered(1))


def _ffn_call(x, p, layer, w_up, conv_w, conv_b, w_down, w_proj, w_gate, ln_g, ln_b, *, alpha, tl,
              natural_out):
    bsz, seq, d = x.shape
    d_ff = w_down.shape[0]
    assert p.shape[-1] == 2 * LANES and tl % PERM_TILE == 0
    ck = 256
    hist = conv_w.shape[0] - 1
    n_sub = tl // PERM_TILE
    kern = functools.partial(_ffn_kernel, d_ff=d_ff, ck=ck, alpha=alpha)
    scratch = [
        pltpu.VMEM((2, n_sub, hist + S_GROUPS, SUBLANES, ck), F32),
        pltpu.VMEM((hist, SUBLANES, 2 * d_ff), F32),
        pltpu.VMEM((tl, d_ff), BF16),
    ]
    if natural_out:
        scratch.append(pltpu.VMEM((d // LANES, tl, LANES), F32))
    return pl.pallas_call(
        kern,
        grid=(bsz, seq // tl),
        in_specs=[
            pl.BlockSpec((1, tl, d), lambda b, t: (b, t, 0)),
            pl.BlockSpec((1, 1, tl, LANES), lambda b, t: (layer, b, t, 0)),
            pl.BlockSpec((1, 1, tl, LANES), lambda b, t: (layer, b, t, 1)),
            _const_spec(w_up.shape), _const_spec(conv_w.shape), _const_spec(conv_b.shape),
            _const_spec(w_down.shape), _const_spec(w_proj.shape), _const_spec(w_gate.shape),
            _const_spec(ln_g.shape), _const_spec(ln_b.shape),
        ],
        out_specs=pl.BlockSpec((1, tl, d), lambda b, t: (b, t, 0)),
        out_shape=jax.ShapeDtypeStruct(x.shape, F32),
        scratch_shapes=scratch,
        compiler_params=pltpu.CompilerParams(
            dimension_semantics=("arbitrary", "arbitrary"), vmem_limit_bytes=VMEM_LIMIT),
        name="conv_ffn",
    )(x, p, p, w_up, conv_w, conv_b, w_down, w_proj, w_gate, ln_g, ln_b)


def _odd_kernel(x_ref, win_ref, cw_ref, wout_ref, g_ref, b_ref, o_ref,
                stage_ref, carry_ref, gated_ref, *, d_c, ck, alpha):
    @pl.when(pl.program_id(1) == 0)
    def _():
        carry_ref[...] = jnp.zeros_like(carry_ref)

    x = x_ref[0]
    xb = x.astype(BF16)
    for c in range(d_c // ck):
        cols = slice(c * ck, (c + 1) * ck)
        bg = _dot(xb, win_ref[:, c * ck:(c + 1) * ck])
        cg = _dot(xb, win_ref[:, d_c + c * ck:d_c + (c + 1) * ck])
        v = _dot(xb, win_ref[:, 2 * d_c + c * ck:2 * d_c + (c + 1) * ck])
        conv = _short_conv(cg * v, stage_ref, carry_ref, cols, cw_ref[:, cols])
        gated_ref[:, cols] = (bg * conv).astype(BF16)
    mix = _dot(gated_ref[...], wout_ref[...])
    o_ref[0] = _layer_norm(alpha * x + mix, g_ref[...], b_ref[...])


def _odd_call(x, w_in, conv_w, w_out, ln_g, ln_b, *, alpha, tl):
    bsz, seq, d = x.shape
    d_c = w_out.shape[0]
    assert tl % PERM_TILE == 0
    ck = 256
    hist = conv_w.shape[0] - 1
    kern = functools.partial(_odd_kernel, d_c=d_c, ck=ck, alpha=alpha)
    return pl.pallas_call(
        kern,
        grid=(bsz, seq // tl),
        in_specs=[
            pl.BlockSpec((1, tl, d), lambda b, t: (b, t, 0)),
            _const_spec(w_in.shape), _const_spec(conv_w.shape), _const_spec(w_out.shape),
            _const_spec(ln_g.shape), _const_spec(ln_b.shape),
        ],
        out_specs=pl.BlockSpec((1, tl, d), lambda b, t: (b, t, 0)),
        out_shape=jax.ShapeDtypeStruct(x.shape, F32),
        scratch_shapes=[
            pltpu.VMEM((tl // PERM_TILE, hist + S_GROUPS, SUBLANES, ck), F32),
            pltpu.VMEM((hist, SUBLANES, d_c), F32),
            pltpu.VMEM((tl, d_c), BF16),
        ],
        compiler_params=pltpu.CompilerParams(
            dimension_semantics=("arbitrary", "arbitrary"), vmem_limit_bytes=VMEM_LIMIT),
        name="short_conv_mixer",
    )(x, w_in, conv_w, w_out, ln_g, ln_b)


def _split3(v):
    hi = v.astype(BF16)
    r1 = v - hi.astype(F32)
    mid = r1.astype(BF16)
    lo = (r1 - mid.astype(F32)).astype(BF16)
    return hi, mid, lo


def _even_kernel(*refs, n_x, d_a, d_b, alpha):
    x_refs = refs[:n_x]
    (win_ref, wdt_ref, caw_ref, cab_ref, lag_ref, lab_ref, cbw_ref, cbb_ref,
     dtb_ref, alog_ref, dskip_ref, nbg_ref, expand_ref, wout_ref, g_ref, b_ref,
     o_ref,
     astage_ref, acarry_ref, bstage_ref, bcarry_ref, ca_ref, xbc_ref, ssd_ref,
     state_ref, cat_ref) = refs[n_x:]
    n_blk = astage_ref.shape[0]
    tl = n_blk * PERM_TILE

    @pl.when(pl.program_id(1) == 0)
    def _():
        acarry_ref[...] = jnp.zeros_like(acarry_ref)
        bcarry_ref[...] = jnp.zeros_like(bcarry_ref)
        state_ref[...] = jnp.zeros_like(state_ref)

    if n_x == 1:
        x = x_refs[0][0]
    else:
        x = _load_sublane_major(x_refs, (0,), n_blk)
    xb = x.astype(BF16)
    ck = astage_ref.shape[3]
    ka = caw_ref.shape[0]
    ahist = ka - 1
    gn = N_GROUPS * N_STATE
    o_z = 2 * d_a
    o_x = o_z + d_b
    lane_blocks = ck // LANES
    row_blocks = S_GROUPS // CONV_ROW_BLOCK

    for c in range(d_a // ck):
        cols = slice(c * ck, (c + 1) * ck)
        lin = _dot(xb, win_ref[:, c * ck:(c + 1) * ck])
        gat = _dot(xb, win_ref[:, d_a + c * ck:d_a + (c + 1) * ck])
        a3 = (lin * _sigmoid(gat)).reshape(tl // SUBLANES, SUBLANES, ck)
        carry = acarry_ref[:, :, cols]
        for u in range(n_blk):
            carry = _stage_history(a3[u * S_GROUPS:(u + 1) * S_GROUPS], astage_ref.at[u], carry, ahist)
        acarry_ref[:, :, cols] = carry

        def conv_rows(it, _, c=c):
            u = it // row_blocks
            r0 = (it % row_blocks) * CONV_ROW_BLOCK
            for kb in range(lane_blocks):
                ls = slice(kb * LANES, (kb + 1) * LANES)
                gl = slice(c * ck + kb * LANES, c * ck + (kb + 1) * LANES)
                acc = astage_ref[u, pl.ds(ahist + r0, CONV_ROW_BLOCK), :, ls] * caw_ref[ka - 1:ka, gl] + cab_ref[:, gl]
                for j in range(ka - 1):
                    acc = acc + astage_ref[u, pl.ds(j + r0, CONV_ROW_BLOCK), :, ls] * caw_ref[j:j + 1, gl]
                ca_ref[pl.ds(u * S_GROUPS + r0, CONV_ROW_BLOCK), :, gl] = acc
            return 0

        lax.fori_loop(0, n_blk * row_blocks, conv_rows, 0)
    ya = _silu(_layer_norm(ca_ref[...].reshape(tl, d_a), lag_ref[...], lab_ref[...]))
    cat_ref[:, 0:d_a] = ya.astype(BF16)

    for c in range((d_b + 2 * gn) // ck):
        cols = slice(c * ck, (c + 1) * ck)
        h = _dot(xb, win_ref[:, o_x + c * ck:o_x + (c + 1) * ck])
        conv = _short_conv(h, bstage_ref, bcarry_ref, cols, cbw_ref[:, cols]) + cbb_ref[:, cols]
        xbc_ref[:, cols] = _silu(conv)

    dt_all = _softplus(_dot(xb, wdt_ref[...]) + dtb_ref[...])
    da_all = dt_all * (-jnp.exp(alog_ref[...]))

    def token_of(r):
        return S_GROUPS * (r % SUBLANES) + r // SUBLANES

    q = PERM_TILE
    causal = (token_of(lax.broadcasted_iota(jnp.int32, (q, q), 0))
              >= token_of(lax.broadcasted_iota(jnp.int32, (q, q), 1)))
    tril = jnp.where(causal, 1.0, 0.0).astype(BF16)
    lane = lax.broadcasted_iota(jnp.int32, (q, LANES), 1)
    half = [lane < HEAD_DIM, lane >= HEAD_DIM]
    pairs_per_group = (d_b // LANES) // N_GROUPS

    for u in range(n_blk):
        rows = slice(u * q, (u + 1) * q)
        dt = dt_all[rows]
        hi, mid, lo = _split3(da_all[rows])
        cum = _dot(tril, hi) + _dot(tril, mid) + _dot(tril, lo)
        cum_t = cum.T
        dt_t = dt.T
        last = cum[q - 1:q, :]
        dstate = jnp.exp(last - cum) * dt
        pad = jnp.zeros((SUBLANES - 1, LANES), F32)
        scal = jnp.concatenate([dstate, jnp.exp(last), pad], axis=0)
        s_hi = scal.astype(BF16)
        s_lo = (scal - s_hi.astype(F32)).astype(BF16)
        expd = _dot(jnp.concatenate([s_hi, s_lo], axis=1), expand_ref[...])
        dstate_x = expd[0:q]
        cdecay_x = expd[q:q + 1]
        st_b = state_ref[...].astype(BF16)

        for g in range(N_GROUPS):
            bm = xbc_ref[rows, d_b + g * N_STATE:d_b + (g + 1) * N_STATE].astype(BF16)
            cm = xbc_ref[rows, d_b + gn + g * N_STATE:d_b + gn + (g + 1) * N_STATE]
            cb = lax.dot_general(cm.astype(BF16), bm, (((1,), (1,)), ((), ())),
                                 preferred_element_type=F32)
            scaled = []
            for pr in range(g * pairs_per_group, (g + 1) * pairs_per_group):
                lanes = slice(pr * LANES, (pr + 1) * LANES)
                xs = xbc_ref[rows, lanes]
                xp = xs.astype(BF16)
                sp = st_b[:, lanes]
                lhs, rhs = [], []
                for j in range(2):
                    hd = 2 * pr + j
                    ccol = jnp.broadcast_to(cum[:, hd:hd + 1], (q, q))
                    crow = jnp.broadcast_to(cum_t[hd:hd + 1, :], (q, q))
                    drow = jnp.broadcast_to(dt_t[hd:hd + 1, :], (q, q))
                    decay = jnp.where(causal, jnp.exp(jnp.where(causal, ccol - crow, 0.0)), 0.0)
                    lhs.append((cb * decay * drow).astype(BF16))
                    lhs.append((cm * jnp.exp(ccol)).astype(BF16))
                    rhs.append(jnp.where(half[j], xp, jnp.zeros_like(xp)))
                    rhs.append(jnp.where(half[j], sp, jnp.zeros_like(sp)))
                y = _dot(jnp.concatenate(lhs, axis=1), jnp.concatenate(rhs, axis=0))
                ssd_ref[rows, lanes] = y + dskip_ref[:, lanes] * xs
                scaled.append((xs * dstate_x[:, lanes]).astype(BF16))
            gcols = slice(g * pairs_per_group * LANES, (g + 1) * pairs_per_group * LANES)
            upd = lax.dot_general(bm, jnp.concatenate(scaled, axis=1), (((0,), (0,)), ((), ())),
                                  preferred_element_type=F32)
            state_ref[:, gcols] = state_ref[:, gcols] * cdecay_x[:, gcols] + upd

    z = _dot(xb, win_ref[:, o_z:o_z + d_b])
    y = ssd_ref[...] * _silu(z)
    yb = y * lax.rsqrt(jnp.mean(y * y, axis=-1, keepdims=True) + LN_EPS) * nbg_ref[...]
    cat_ref[:, d_a:d_a + d_b] = yb.astype(BF16)

    mix = _dot(cat_ref[...], wout_ref[...])
    o_ref[0] = _layer_norm(alpha * x + mix, g_ref[...], b_ref[...])


def _even_call(x, w_main, w_dt, conv_a_w, conv_a_b, ln_a_g, ln_a_b, conv_b_w, conv_b_b,
               dt_bias, a_log, d_skip_x, norm_b_g, expand, w_out, ln_g, ln_b, *, alpha, tl, natural_in):
    bsz, seq, d = x.shape
    d_a = conv_a_w.shape[1]
    d_b = norm_b_g.shape[1]
    xbc_dim = conv_b_w.shape[1]
    ck = 256
    ahist = conv_a_w.shape[0] - 1
    bhist = conv_b_w.shape[0] - 1
    n_blk = tl // PERM_TILE
    assert tl % PERM_TILE == 0 and PERM_TILE == N_STATE
    consts = (w_main, w_dt, conv_a_w, conv_a_b, ln_a_g, ln_a_b, conv_b_w, conv_b_b,
              dt_bias, a_log, d_skip_x, norm_b_g, expand, w_out, ln_g, ln_b)
    if natural_in:
        n_x = d // LANES
        x_specs = [pl.BlockSpec((1, tl, LANES), functools.partial(lambda b, t, kb: (b, t, kb), kb=kb))
                   for kb in range(n_x)]
    else:
        n_x = 1
        x_specs = [pl.BlockSpec((1, tl, d), lambda b, t: (b, t, 0))]
    kern = functools.partial(_even_kernel, n_x=n_x, d_a=d_a, d_b=d_b, alpha=alpha)
    return pl.pallas_call(
        kern,
        grid=(bsz, seq // tl),
        in_specs=x_specs + [_const_spec(c.shape) for c in consts],
        out_specs=pl.BlockSpec((1, tl, d), lambda b, t: (b, t, 0)),
        out_shape=jax.ShapeDtypeStruct(x.shape, F32),
        scratch_shapes=[
            pltpu.VMEM((n_blk, ahist + S_GROUPS, SUBLANES, ck), F32),
            pltpu.VMEM((ahist, SUBLANES, d_a), F32),
            pltpu.VMEM((n_blk, bhist + S_GROUPS, SUBLANES, ck), F32),
            pltpu.VMEM((bhist, SUBLANES, xbc_dim), F32),
            pltpu.VMEM((tl // SUBLANES, SUBLANES, d_a), F32),
            pltpu.VMEM((tl, xbc_dim), F32),
            pltpu.VMEM((tl, d_b), F32),
            pltpu.VMEM((N_STATE, d_b), F32),
            pltpu.VMEM((tl, d_a + d_b), BF16),
        ],
        compiler_params=pltpu.CompilerParams(
            dimension_semantics=("arbitrary", "arbitrary"), vmem_limit_bytes=VMEM_LIMIT),
        name="conformer_ssd_mixer",
    )(*([x] * n_x), *consts)


def _row(v):
    return v.reshape(1, -1).astype(F32)


def _pad_lanes(v):
    return jnp.pad(v, ((0, 0), (0, LANES - v.shape[1])))


def kernel(x, p, e_w_in, e_conv_a_w, e_conv_a_b, e_ln_a_g, e_ln_a_b, e_conv_b_w, e_conv_b_b, e_dt_bias, e_a_log, e_d_skip, e_norm_b_g, e_w_out, o_w_in, o_conv_w, o_w_out, f_w_up, f_conv_w, f_conv_b, f_w_down, ple_w_proj, ple_w_gate, ln_g, ln_b):
    depth = f_w_up.shape[0]
    alpha = (2.0 * depth) ** 0.25
    d_b = e_norm_b_g.shape[1]
    n_heads = e_dt_bias.shape[1]
    o_dt = e_w_in.shape[2] - n_heads
    tl_even = min(256, x.shape[1])
    tl_odd = tl_ffn = min(512, x.shape[1])

    head_of_col = jnp.arange(d_b) // HEAD_DIM
    sel = (jnp.arange(LANES)[:, None] == head_of_col[None, :]).astype(BF16)
    expand = jnp.concatenate([sel, sel], axis=0)

    for i in range(depth):
        j = i // 2
        if i % 2 == 0:
            w_in = e_w_in[j]
            x = _even_call(
                x, w_in[:, :o_dt].astype(BF16), _pad_lanes(w_in[:, o_dt:]).astype(BF16),
                e_conv_a_w[j], _row(e_conv_a_b[j]), _row(e_ln_a_g[j]), _row(e_ln_a_b[j]),
                e_conv_b_w[j], _row(e_conv_b_b[j]),
                _pad_lanes(_row(e_dt_bias[j])), _pad_lanes(_row(e_a_log[j])),
                _row(jnp.repeat(e_d_skip[j], HEAD_DIM)), _row(e_norm_b_g[j]), expand,
                e_w_out[j].astype(BF16), _row(ln_g[i, 0]), _row(ln_b[i, 0]),
                alpha=alpha, tl=tl_even, natural_in=(i == 0))
        else:
            x = _odd_call(x, o_w_in[j].astype(BF16), o_conv_w[j], o_w_out[j].astype(BF16),
                          _row(ln_g[i, 0]), _row(ln_b[i, 0]), alpha=alpha, tl=tl_odd)
        x = _ffn_call(x, p, i, f_w_up[i].astype(BF16), f_conv_w[i], _row(f_conv_b[i]),
                      f_w_down[i].astype(BF16), ple_w_proj[i].astype(BF16), ple_w_gate[i].astype(BF16),
                      _row(ln_g[i, 1]), _row(ln_b[i, 1]), alpha=alpha, tl=tl_ffn,
                      natural_out=(i == depth - 1))
    return x
```

```python
import functools

import jax
import jax.numpy as jnp
from jax import lax
from jax.experimental import pallas as pl
from jax.experimental.pallas import tpu as pltpu

F32 = jnp.float32
BF16 = jnp.bfloat16

LN_EPS = 1e-5
LOG2E = 1.4426950408889634
HEAD_DIM = 64
N_GROUPS = 4
N_STATE = 128
LANES = 128
SUBLANES = 8
PERM_TILE = 128
S_GROUPS = PERM_TILE // SUBLANES
CONV_ROW_BLOCK = 16
PROJ_PER_CHUNK = 4
COL_CHUNK = 256
VMEM_LIMIT = 60 * 1024 * 1024


def _dot(a, b):
    return jnp.dot(a, b, preferred_element_type=F32)


def _layer_norm(v, g, b):
    mu = jnp.mean(v, axis=-1, keepdims=True)
    d = v - mu
    var = jnp.mean(d * d, axis=-1, keepdims=True)
    return d * lax.rsqrt(var + LN_EPS) * g + b


def _sigmoid(v):
    return 1.0 / (1.0 + jnp.exp(-v))


def _silu(v):
    return v * _sigmoid(v)


def _softplus(v):
    return jnp.maximum(v, 0.0) + jnp.log(1.0 + jnp.exp(-jnp.abs(v)))


def _load_sublane_major(refs, lead, n_blocks):
    groups = []
    for u in range(n_blocks):
        for i in range(S_GROUPS):
            rows = pl.ds(u * PERM_TILE + i, SUBLANES, stride=S_GROUPS)
            groups.append(jnp.concatenate([r[lead + (rows, slice(None))] for r in refs], axis=1))
    return jnp.concatenate(groups, axis=0)


def _stage_history(h3, stage_ref, carry, hist):
    n1 = min(hist, S_GROUPS)
    n2 = hist - n1
    assert n2 <= S_GROUPS
    rolled = pltpu.roll(h3[S_GROUPS - n1:], 1, axis=1)
    if n2:
        rolled = jnp.concatenate([pltpu.roll(h3[S_GROUPS - n2:], 2, axis=1), rolled], axis=0)
    sub = lax.broadcasted_iota(jnp.int32, rolled.shape, 1)
    grp = lax.broadcasted_iota(jnp.int32, rolled.shape, 0)
    from_prev = sub < jnp.where(grp < n2, 2, 1)
    stage_ref[0:hist] = jnp.where(from_prev, carry, rolled)
    stage_ref[hist:hist + S_GROUPS] = h3
    return rolled


def _short_conv(h, stage_ref, carry_ref, cols, w):
    tl, n = h.shape
    k = w.shape[0]
    hist = k - 1
    h3 = h.reshape(tl // SUBLANES, SUBLANES, n)
    carry = carry_ref[:, :, cols]
    outs = []
    for u in range(tl // PERM_TILE):
        hu = h3[u * S_GROUPS:(u + 1) * S_GROUPS]
        carry = _stage_history(hu, stage_ref.at[u], carry, hist)
        out = hu * w[k - 1:k, :]
        for j in range(k - 1):
            out = out + stage_ref[u, j:j + S_GROUPS] * w[j:j + 1, :]
        outs.append(out)
    carry_ref[:, :, cols] = carry
    return jnp.concatenate(outs, axis=0).reshape(tl, n)


def _const_spec(shape):
    nd = len(shape)
    return pl.BlockSpec(shape, lambda b, t: (0,) * nd, pipeline_mode=pl.Buffered(1))


def _ffn_body(x, p_refs, wup_ref, cw_ref, cb_ref, wdn_ref, wproj_ref, wgate_ref, g_ref, b_ref,
              stage_ref, carry_ref, act_ref, *, alpha):
    tl = x.shape[0]
    d_ff = wdn_ref.shape[0]
    ck = COL_CHUNK
    xb = x.astype(BF16)

    def conv_cols(col0, slot):
        cols = slice(col0, col0 + ck)
        h = _dot(xb, wup_ref[:, cols])
        return _short_conv(h, stage_ref.at[slot], carry_ref, cols, cw_ref[:, cols]) + cb_ref[:, cols]

    for c in range(d_ff // ck):
        ha = conv_cols(c * ck, 0)
        hg = conv_cols(d_ff + c * ck, 1)
        act_ref[:, c * ck:(c + 1) * ck] = (_silu(ha) * hg).astype(BF16)

    ffn = _dot(act_ref[...], wdn_ref[...])
    gate = _sigmoid(_dot(xb, wgate_ref[...]))
    pe = _load_sublane_major(p_refs, (0, 0), tl // PERM_TILE)
    ple = _dot(pe.astype(BF16), wproj_ref[...]) * gate
    return _layer_norm(alpha * x + ffn + ple, g_ref[...], b_ref[...])


def _ffn_scratch(tl, d_ff, hist):
    return [
        pltpu.VMEM((2, tl // PERM_TILE, hist + S_GROUPS, SUBLANES, COL_CHUNK), F32),
        pltpu.VMEM((hist, SUBLANES, 2 * d_ff), F32),
        pltpu.VMEM((tl, d_ff), BF16),
    ]


def _ffn_kernel(x_ref, p0_ref, p1_ref, wup_ref, cw_ref, cb_ref, wdn_ref, wproj_ref, wgate_ref,
                g_ref, b_ref, o_ref, stage_ref, carry_ref, act_ref, *unperm, alpha):
    @pl.when(pl.program_id(1) == 0)
    def _():
        carry_ref[...] = jnp.zeros_like(carry_ref)

    x = x_ref[0]
    tl, d = x.shape
    y = _ffn_body(x, (p0_ref, p1_ref), wup_ref, cw_ref, cb_ref, wdn_ref, wproj_ref, wgate_ref,
                  g_ref, b_ref, stage_ref, carry_ref, act_ref, alpha=alpha)
    if not unperm:
        o_ref[0] = y
    else:
        nat_ref, = unperm
        y3 = y.reshape(tl // SUBLANES, SUBLANES, d)
        for u in range(tl // PERM_TILE):
            for i in range(S_GROUPS):
                rows = pl.ds(u * PERM_TILE + i, SUBLANES, stride=S_GROUPS)
                for kb in range(d // LANES):
                    nat_ref[kb, rows, :] = y3[u * S_GROUPS + i, :, kb * LANES:(kb + 1) * LANES]
        for kb in range(d // LANES):
            o_ref[0, :, kb * LANES:(kb + 1) * LANES] = nat_ref[kb]


def _ffn_call(x, p, layer, ffn_consts, *, alpha, tl, natural_out):
    bsz, seq, d = x.shape
    d_ff = ffn_consts[3].shape[0]
    hist = ffn_consts[1].shape[0] - 1
    assert p.shape[-1] == 2 * LANES and tl % PERM_TILE == 0
    scratch = _ffn_scratch(tl, d_ff, hist)
    if natural_out:
        scratch.append(pltpu.VMEM((d // LANES, tl, LANES), F32))
    return pl.pallas_call(
        functools.partial(_ffn_kernel, alpha=alpha),
        grid=(bsz, seq // tl),
        in_specs=[
            pl.BlockSpec((1, tl, d), lambda b, t: (b, t, 0)),
            pl.BlockSpec((1, 1, tl, LANES), lambda b, t: (layer, b, t, 0)),
            pl.BlockSpec((1, 1, tl, LANES), lambda b, t: (layer, b, t, 1)),
        ] + [_const_spec(c.shape) for c in ffn_consts],
        out_specs=pl.BlockSpec((1, tl, d), lambda b, t: (b, t, 0)),
        out_shape=jax.ShapeDtypeStruct(x.shape, F32),
        scratch_shapes=scratch,
        compiler_params=pltpu.CompilerParams(
            dimension_semantics=("arbitrary", "arbitrary"), vmem_limit_bytes=VMEM_LIMIT),
        name="conv_ffn",
    )(x, p, p, *ffn_consts)


def _odd_kernel(x_ref, win_ref, cw_ref, wout_ref, g_ref, b_ref, o_ref,
                stage_ref, carry_ref, gated_ref, *, alpha):
    @pl.when(pl.program_id(1) == 0)
    def _():
        carry_ref[...] = jnp.zeros_like(carry_ref)

    x = x_ref[0]
    xb = x.astype(BF16)
    d_c = wout_ref.shape[0]
    ck = COL_CHUNK
    for c in range(d_c // ck):
        cols = slice(c * ck, (c + 1) * ck)
        bg = _dot(xb, win_ref[:, c * ck:(c + 1) * ck])
        cg = _dot(xb, win_ref[:, d_c + c * ck:d_c + (c + 1) * ck])
        v = _dot(xb, win_ref[:, 2 * d_c + c * ck:2 * d_c + (c + 1) * ck])
        conv = _short_conv(cg * v, stage_ref, carry_ref, cols, cw_ref[:, cols])
        gated_ref[:, cols] = (bg * conv).astype(BF16)
    mix = _dot(gated_ref[...], wout_ref[...])
    o_ref[0] = _layer_norm(alpha * x + mix, g_ref[...], b_ref[...])


def _odd_call(x, w_in, conv_w, w_out, ln_g, ln_b, *, alpha, tl):
    bsz, seq, d = x.shape
    d_c = w_out.shape[0]
    assert tl % PERM_TILE == 0
    hist = conv_w.shape[0] - 1
    return pl.pallas_call(
        functools.partial(_odd_kernel, alpha=alpha),
        grid=(bsz, seq // tl),
        in_specs=[
            pl.BlockSpec((1, tl, d), lambda b, t: (b, t, 0)),
            _const_spec(w_in.shape), _const_spec(conv_w.shape), _const_spec(w_out.shape),
            _const_spec(ln_g.shape), _const_spec(ln_b.shape),
        ],
        out_specs=pl.BlockSpec((1, tl, d), lambda b, t: (b, t, 0)),
        out_shape=jax.ShapeDtypeStruct(x.shape, F32),
        scratch_shapes=[
            pltpu.VMEM((tl // PERM_TILE, hist + S_GROUPS, SUBLANES, COL_CHUNK), F32),
            pltpu.VMEM((hist, SUBLANES, d_c), F32),
            pltpu.VMEM((tl, d_c), BF16),
        ],
        compiler_params=pltpu.CompilerParams(
            dimension_semantics=("arbitrary", "arbitrary"), vmem_limit_bytes=VMEM_LIMIT),
        name="short_conv_mixer",
    )(x, w_in, conv_w, w_out, ln_g, ln_b)


def _proj_schedule(n_chunks, n_blocks, slots_per_chunk):
    rest = list(range(3 * n_chunks, n_blocks)) + list(range(2 * n_chunks, 3 * n_chunks))
    order = [0, n_chunks]
    for c in range(n_chunks):
        step = [c + 1, n_chunks + c + 1] if c + 1 < n_chunks else []
        while len(step) < slots_per_chunk and rest:
            step.append(rest.pop(0))
        assert len(step) == slots_per_chunk
        order += step
    return order + rest


def _split3(v):
    hi = v.astype(BF16)
    r1 = v - hi.astype(F32)
    mid = r1.astype(BF16)
    lo = (r1 - mid.astype(F32)).astype(BF16)
    return hi, mid, lo


def _mixer_body(x, win_ref, wdt_ref, caw_ref, cab_ref, lag_ref, lab_ref, cbw_ref, cbb_ref,
                dtb_ref, alog_ref, dskip_ref, nbg_ref, expand_ref, wout_ref, g_ref, b_ref,
                astage_ref, acarry_ref, bstage_ref, bcarry_ref, ca_ref, xbc_ref, ssd_ref,
                state_ref, cat_ref, xb_ref, proj_ref, *, alpha):
    tl = x.shape[0]
    n_blk = tl // PERM_TILE
    d_a = caw_ref.shape[1]
    d_b = nbg_ref.shape[1]
    ck = COL_CHUNK
    ka = caw_ref.shape[0]
    ahist = ka - 1
    gn = N_GROUPS * N_STATE
    n_chunks = d_a // ck
    z_blk = 2 * n_chunks
    xbc_blk = z_blk + d_b // ck
    lane_blocks = ck // LANES
    row_blocks = S_GROUPS // CONV_ROW_BLOCK
    conv_steps = n_blk * row_blocks

    xb_ref[...] = x.astype(BF16)
    pos_of = {blk: pos for pos, blk in enumerate(_proj_schedule(n_chunks, win_ref.shape[0], PROJ_PER_CHUNK))}
    n_ahead = 2 + n_chunks * PROJ_PER_CHUNK

    def project(pos):
        proj_ref[pos] = _dot(xb_ref[...], win_ref[pos])

    project(0)
    project(1)

    def stage_chunk(c):
        cols = slice(c * ck, (c + 1) * ck)
        a3 = (proj_ref[pos_of[c]] * _sigmoid(proj_ref[pos_of[n_chunks + c]])).reshape(tl // SUBLANES, SUBLANES, ck)
        stage = astage_ref.at[c % astage_ref.shape[0]]
        carry = acarry_ref[:, :, cols]
        for u in range(n_blk):
            carry = _stage_history(a3[u * S_GROUPS:(u + 1) * S_GROUPS], stage.at[u], carry, ahist)
        acarry_ref[:, :, cols] = carry

    def conv_chunk(c):
        stage = astage_ref.at[c % astage_ref.shape[0]]
        for it in range(conv_steps):
            u = it // row_blocks
            r0 = (it % row_blocks) * CONV_ROW_BLOCK
            for kb in range(lane_blocks):
                ls = slice(kb * LANES, (kb + 1) * LANES)
                gl = slice(c * ck + kb * LANES, c * ck + (kb + 1) * LANES)
                acc = stage[u, ahist + r0:ahist + r0 + CONV_ROW_BLOCK, :, ls] * caw_ref[ka - 1:ka, gl] + cab_ref[:, gl]
                for j in range(ka - 1):
                    acc = acc + stage[u, j + r0:j + r0 + CONV_ROW_BLOCK, :, ls] * caw_ref[j:j + 1, gl]
                g0 = u * S_GROUPS + r0
                ca_ref[g0:g0 + CONV_ROW_BLOCK, :, gl] = acc
        for k in range(PROJ_PER_CHUNK):
            project(2 + c * PROJ_PER_CHUNK + k)

    for c in range(n_chunks):
        stage_chunk(c)
        pl.when(pl.program_id(0) >= 0)(functools.partial(conv_chunk, c))
    for pos in range(n_ahead, win_ref.shape[0]):
        project(pos)
    ya = _silu(_layer_norm(ca_ref[...].reshape(tl, d_a), lag_ref[...], lab_ref[...]))
    cat_ref[:, 0:d_a] = ya.astype(BF16)

    for c in range((d_b + 2 * gn) // ck):
        cols = slice(c * ck, (c + 1) * ck)
        h = proj_ref[pos_of[xbc_blk + c]]
        conv = _short_conv(h, bstage_ref, bcarry_ref, cols, cbw_ref[:, cols]) + cbb_ref[:, cols]
        xbc_ref[:, cols] = _silu(conv)

    dt_all = _softplus(_dot(xb_ref[...], wdt_ref[...]) + dtb_ref[...])
    da_all = dt_all * (-LOG2E * jnp.exp(alog_ref[...]))

    def token_of(r):
        return S_GROUPS * (r % SUBLANES) + r // SUBLANES

    q = PERM_TILE
    causal = (token_of(lax.broadcasted_iota(jnp.int32, (q, q), 0))
              >= token_of(lax.broadcasted_iota(jnp.int32, (q, q), 1)))
    tril = jnp.where(causal, 1.0, 0.0).astype(BF16)
    lane = lax.broadcasted_iota(jnp.int32, (q, LANES), 1)
    half = [lane < HEAD_DIM, lane >= HEAD_DIM]
    pairs_per_group = (d_b // LANES) // N_GROUPS

    for u in range(n_blk):
        rows = slice(u * q, (u + 1) * q)
        dt = dt_all[rows]
        hi, mid, lo = _split3(da_all[rows])
        cum = _dot(tril, hi) + _dot(tril, mid) + _dot(tril, lo)
        src_t = (cum - jnp.log(dt) * LOG2E).T
        last = cum[q - 1:q, :]
        dstate = jnp.exp2(last - cum) * dt
        pad = jnp.zeros((SUBLANES - 1, LANES), F32)
        scal = jnp.concatenate([dstate, jnp.exp2(last), pad], axis=0)
        s_hi = scal.astype(BF16)
        s_lo = (scal - s_hi.astype(F32)).astype(BF16)
        expd = _dot(jnp.concatenate([s_hi, s_lo], axis=1), expand_ref[...])
        dstate_x = expd[0:q]
        cdecay_x = expd[q:q + 1]
        st_b = state_ref[...].astype(BF16)

        for g in range(N_GROUPS):
            bm = xbc_ref[rows, d_b + g * N_STATE:d_b + (g + 1) * N_STATE].astype(BF16)
            cm = xbc_ref[rows, d_b + gn + g * N_STATE:d_b + gn + (g + 1) * N_STATE]
            cb = lax.dot_general(cm.astype(BF16), bm, (((1,), (1,)), ((), ())),
                                 preferred_element_type=F32)
            scaled = []
            for pr in range(g * pairs_per_group, (g + 1) * pairs_per_group):
                lanes = slice(pr * LANES, (pr + 1) * LANES)
                xs = xbc_ref[rows, lanes]
                xp = xs.astype(BF16)
                sp = st_b[:, lanes]
                lhs, rhs = [], []
                for j in range(2):
                    hd = 2 * pr + j
                    ccol = jnp.broadcast_to(cum[:, hd:hd + 1], (q, q))
                    srow = jnp.broadcast_to(src_t[hd:hd + 1, :], (q, q))
                    lhs.append((cb * jnp.where(causal, jnp.exp2(ccol - srow), 0.0)).astype(BF16))
                    lhs.append((cm * jnp.exp2(ccol)).astype(BF16))
                    rhs.append(jnp.where(half[j], xp, jnp.zeros_like(xp)))
                    rhs.append(jnp.where(half[j], sp, jnp.zeros_like(sp)))
                y = _dot(jnp.concatenate(lhs, axis=1), jnp.concatenate(rhs, axis=0))
                ssd_ref[rows, lanes] = y + dskip_ref[:, lanes] * xs
                scaled.append((xs * dstate_x[:, lanes]).astype(BF16))
            gcols = slice(g * pairs_per_group * LANES, (g + 1) * pairs_per_group * LANES)
            upd = lax.dot_general(bm, jnp.concatenate(scaled, axis=1), (((0,), (0,)), ((), ())),
                                  preferred_element_type=F32)
            state_ref[:, gcols] = state_ref[:, gcols] * cdecay_x[:, gcols] + upd

    z = jnp.concatenate([proj_ref[pos_of[z_blk + c]] for c in range(d_b // ck)], axis=1)
    y = ssd_ref[...] * _silu(z)
    yb = y * lax.rsqrt(jnp.mean(y * y, axis=-1, keepdims=True) + LN_EPS) * nbg_ref[...]
    cat_ref[:, d_a:d_a + d_b] = yb.astype(BF16)

    mix = _dot(cat_ref[...], wout_ref[...])
    return _layer_norm(alpha * x + mix, g_ref[...], b_ref[...])


N_MIXER_CONSTS = 16


def _even_kernel(*refs, n_x, alpha):
    x_refs = refs[:n_x]
    mixer_consts = refs[n_x:n_x + N_MIXER_CONSTS]
    o_ref = refs[n_x + N_MIXER_CONSTS]
    scratch = refs[n_x + N_MIXER_CONSTS + 1:]
    acarry_ref, bcarry_ref, state_ref = scratch[1], scratch[3], scratch[7]
    tl = o_ref.shape[1]

    @pl.when(pl.program_id(1) == 0)
    def _():
        for r in (acarry_ref, bcarry_ref, state_ref):
            r[...] = jnp.zeros_like(r)

    if n_x == 1:
        x = x_refs[0][0]
    else:
        x = _load_sublane_major(x_refs, (0,), tl // PERM_TILE)
    o_ref[0] = _mixer_body(x, *mixer_consts, *scratch, alpha=alpha)


def _even_call(x, mixer_consts, *, alpha, tl, natural_in):
    bsz, seq, d = x.shape
    assert len(mixer_consts) == N_MIXER_CONSTS
    w_main, conv_a_w, conv_b_w, norm_b_g = mixer_consts[0], mixer_consts[2], mixer_consts[6], mixer_consts[11]
    d_a = conv_a_w.shape[1]
    d_b = norm_b_g.shape[1]
    xbc_dim = conv_b_w.shape[1]
    ck = COL_CHUNK
    ahist = conv_a_w.shape[0] - 1
    bhist = conv_b_w.shape[0] - 1
    n_blk = tl // PERM_TILE
    assert tl % PERM_TILE == 0 and PERM_TILE == N_STATE and d_a == d_b
    n_proj = w_main.shape[1] // ck
    order = _proj_schedule(d_a // ck, n_proj, PROJ_PER_CHUNK)
    w_main = w_main.reshape(d, n_proj, ck).transpose(1, 0, 2)[jnp.array(order)]
    mixer_consts = (w_main,) + tuple(mixer_consts[1:])

    if natural_in:
        n_x = d // LANES
        x_specs = [pl.BlockSpec((1, tl, LANES), functools.partial(lambda b, t, kb: (b, t, kb), kb=kb))
                   for kb in range(n_x)]
    else:
        n_x = 1
        x_specs = [pl.BlockSpec((1, tl, d), lambda b, t: (b, t, 0))]
    mixer_scratch = [
        pltpu.VMEM((2, n_blk, ahist + S_GROUPS, SUBLANES, ck), F32),
        pltpu.VMEM((ahist, SUBLANES, d_a), F32),
        pltpu.VMEM((n_blk, bhist + S_GROUPS, SUBLANES, ck), F32),
        pltpu.VMEM((bhist, SUBLANES, xbc_dim), F32),
        pltpu.VMEM((tl // SUBLANES, SUBLANES, d_a), F32),
        pltpu.VMEM((tl, xbc_dim), F32),
        pltpu.VMEM((tl, d_b), F32),
        pltpu.VMEM((N_STATE, d_b), F32),
        pltpu.VMEM((tl, d_a + d_b), BF16),
        pltpu.VMEM((tl, d), BF16),
        pltpu.VMEM((n_proj, tl, ck), F32),
    ]
    return pl.pallas_call(
        functools.partial(_even_kernel, n_x=n_x, alpha=alpha),
        grid=(bsz, seq // tl),
        in_specs=x_specs + [_const_spec(c.shape) for c in mixer_consts],
        out_specs=pl.BlockSpec((1, tl, d), lambda b, t: (b, t, 0)),
        out_shape=jax.ShapeDtypeStruct(x.shape, F32),
        scratch_shapes=mixer_scratch,
        compiler_params=pltpu.CompilerParams(
            dimension_semantics=("arbitrary", "arbitrary"), vmem_limit_bytes=VMEM_LIMIT),
        name="conformer_ssd_mixer",
    )(*([x] * n_x), *mixer_consts)


def _row(v):
    return v.reshape(1, -1).astype(F32)


def _pad_lanes(v):
    return jnp.pad(v, ((0, 0), (0, LANES - v.shape[1])))


def kernel(x, p, e_w_in, e_conv_a_w, e_conv_a_b, e_ln_a_g, e_ln_a_b, e_conv_b_w, e_conv_b_b, e_dt_bias, e_a_log, e_d_skip, e_norm_b_g, e_w_out, o_w_in, o_conv_w, o_w_out, f_w_up, f_conv_w, f_conv_b, f_w_down, ple_w_proj, ple_w_gate, ln_g, ln_b):
    depth = f_w_up.shape[0]
    alpha = (2.0 * depth) ** 0.25
    d_b = e_norm_b_g.shape[1]
    n_heads = e_dt_bias.shape[1]
    o_dt = e_w_in.shape[2] - n_heads
    tl_even = tl_odd = tl_ffn = min(512, x.shape[1])

    head_of_col = jnp.arange(d_b) // HEAD_DIM
    sel = (jnp.arange(LANES)[:, None] == head_of_col[None, :]).astype(BF16)
    expand = jnp.concatenate([sel, sel], axis=0)

    for i in range(depth):
        j = i // 2
        ffn_consts = (f_w_up[i].astype(BF16), f_conv_w[i], _row(f_conv_b[i]), f_w_down[i].astype(BF16),
                      ple_w_proj[i].astype(BF16), ple_w_gate[i].astype(BF16), _row(ln_g[i, 1]), _row(ln_b[i, 1]))
        if i % 2 == 0:
            w_in = e_w_in[j]
            mixer_consts = (
                w_in[:, :o_dt].astype(BF16), _pad_lanes(w_in[:, o_dt:]).astype(BF16),
                e_conv_a_w[j], _row(e_conv_a_b[j]), _row(e_ln_a_g[j]), _row(e_ln_a_b[j]),
                e_conv_b_w[j], _row(e_conv_b_b[j]),
                _pad_lanes(_row(e_dt_bias[j])), _pad_lanes(_row(e_a_log[j])),
                _row(jnp.repeat(e_d_skip[j], HEAD_DIM)), _row(e_norm_b_g[j]), expand,
                e_w_out[j].astype(BF16), _row(ln_g[i, 0]), _row(ln_b[i, 0]))
            x = _even_call(x, mixer_consts, alpha=alpha, tl=tl_even, natural_in=(i == 0))
        else:
            x = _odd_call(x, o_w_in[j].astype(BF16), o_conv_w[j], o_w_out[j].astype(BF16),
                          _row(ln_g[i, 0]), _row(ln_b[i, 0]), alpha=alpha, tl=tl_odd)
        x = _ffn_call(x, p, i, ffn_consts, alpha=alpha, tl=tl_ffn, natural_out=(i == depth - 1))
    return x
```

```python
import functools

import jax
import jax.numpy as jnp
from jax import lax
from jax.experimental import pallas as pl
from jax.experimental.pallas import tpu as pltpu

F32 = jnp.float32
BF16 = jnp.bfloat16

LN_EPS = 1e-5
LOG2E = 1.4426950408889634
HEAD_DIM = 64
N_GROUPS = 4
N_STATE = 128
LANES = 128
SUBLANES = 8
PERM_TILE = 128
S_GROUPS = PERM_TILE // SUBLANES
CONV_ROW_BLOCK = 16
PROJ_PER_CHUNK = 4
COL_CHUNK = 256
VMEM_LIMIT = 60 * 1024 * 1024


def _dot(a, b):
    return jnp.dot(a, b, preferred_element_type=F32)


def _layer_norm(v, g, b):
    mu = jnp.mean(v, axis=-1, keepdims=True)
    d = v - mu
    var = jnp.mean(d * d, axis=-1, keepdims=True)
    return d * lax.rsqrt(var + LN_EPS) * g + b


def _sigmoid(v):
    return 1.0 / (1.0 + jnp.exp(-v))


def _silu(v):
    return v * _sigmoid(v)


def _softplus(v):
    return jnp.maximum(v, 0.0) + jnp.log(1.0 + jnp.exp(-jnp.abs(v)))


def _load_sublane_major(refs, lead, n_blocks):
    groups = []
    for u in range(n_blocks):
        for i in range(S_GROUPS):
            rows = pl.ds(u * PERM_TILE + i, SUBLANES, stride=S_GROUPS)
            groups.append(jnp.concatenate([r[lead + (rows, slice(None))] for r in refs], axis=1))
    return jnp.concatenate(groups, axis=0)


def _stage_history(h3, stage_ref, carry, hist):
    n1 = min(hist, S_GROUPS)
    n2 = hist - n1
    assert n2 <= S_GROUPS
    rolled = pltpu.roll(h3[S_GROUPS - n1:], 1, axis=1)
    if n2:
        rolled = jnp.concatenate([pltpu.roll(h3[S_GROUPS - n2:], 2, axis=1), rolled], axis=0)
    sub = lax.broadcasted_iota(jnp.int32, rolled.shape, 1)
    grp = lax.broadcasted_iota(jnp.int32, rolled.shape, 0)
    from_prev = sub < jnp.where(grp < n2, 2, 1)
    stage_ref[0:hist] = jnp.where(from_prev, carry, rolled)
    stage_ref[hist:hist + S_GROUPS] = h3
    return rolled


def _short_conv(h, stage_ref, carry_ref, cols, w):
    tl, n = h.shape
    k = w.shape[0]
    hist = k - 1
    h3 = h.reshape(tl // SUBLANES, SUBLANES, n)
    carry = carry_ref[:, :, cols]
    outs = []
    for u in range(tl // PERM_TILE):
        hu = h3[u * S_GROUPS:(u + 1) * S_GROUPS]
        carry = _stage_history(hu, stage_ref.at[u], carry, hist)
        out = hu * w[k - 1:k, :]
        for j in range(k - 1):
            out = out + stage_ref[u, j:j + S_GROUPS] * w[j:j + 1, :]
        outs.append(out)
    carry_ref[:, :, cols] = carry
    return jnp.concatenate(outs, axis=0).reshape(tl, n)


def _const_spec(shape):
    nd = len(shape)
    return pl.BlockSpec(shape, lambda b, t: (0,) * nd, pipeline_mode=pl.Buffered(1))


BF16_SUBLANES = 16


def _cast_plan(jobs, n_b, n_t):
    steps = n_b * n_t
    in_specs, out_specs, out_shapes = [], [], []
    for w, layer, blocked in jobs:
        _, r, c = w.shape
        slabs = max(k for k in range(1, steps + 1)
                    if steps % k == 0 and r % k == 0 and (r // k) % BF16_SUBLANES == 0)
        per = steps // slabs
        rows = r // slabs
        in_specs.append(pl.BlockSpec(
            (1, rows, c), functools.partial(lambda b, t, layer, per: (layer, (b * n_t + t) // per, 0),
                                            layer=layer, per=per)))
        row_slab = functools.partial(lambda b, t, per: ((b * n_t + t) // per, 0), per=per)
        if not blocked:
            out_specs.append(pl.BlockSpec((rows, c), row_slab))
            out_shapes.append(jax.ShapeDtypeStruct((r, c), BF16))
        else:
            nb = c // COL_CHUNK
            assert 0 < c - nb * COL_CHUNK <= LANES
            out_specs.append(pl.BlockSpec(
                (nb, rows, COL_CHUNK), functools.partial(lambda b, t, per: (0, (b * n_t + t) // per, 0), per=per)))
            out_shapes.append(jax.ShapeDtypeStruct((nb, r, COL_CHUNK), BF16))
            out_specs.append(pl.BlockSpec((rows, LANES), row_slab))
            out_shapes.append(jax.ShapeDtypeStruct((r, LANES), BF16))
    return in_specs, out_specs, out_shapes


def _blocked_bf16(w):
    r, c = w.shape
    nb = c // COL_CHUNK
    wb = w.astype(BF16)
    main = wb[:, :nb * COL_CHUNK].reshape(r, nb, COL_CHUNK).transpose(1, 0, 2)
    return main, jnp.pad(wb[:, nb * COL_CHUNK:], ((0, 0), (0, LANES - (c - nb * COL_CHUNK))))


def _split_refs(refs, n_in, blocked):
    n_src = len(blocked)
    n_dst = sum(2 if b else 1 for b in blocked)
    k = n_in + n_src
    return refs[:n_in], refs[n_in:k], refs[k], refs[k + 1:k + 1 + n_dst], refs[k + 1 + n_dst:]


def _run_casts(blocked_flags, src_refs, dst_refs):
    dst = list(dst_refs)
    for blocked, src in zip(blocked_flags, src_refs):
        w = src[0].astype(BF16)
        if not blocked:
            dst.pop(0)[...] = w
        else:
            main, rest = dst.pop(0), dst.pop(0)
            nb = main.shape[0]
            for blk in range(nb):
                main[blk] = w[:, blk * COL_CHUNK:(blk + 1) * COL_CHUNK]
            tail = w[:, nb * COL_CHUNK:]
            rest[...] = jnp.concatenate(
                [tail, jnp.zeros((tail.shape[0], LANES - tail.shape[1]), BF16)], axis=1)


def _ffn_body(x, p_refs, wup_ref, cw_ref, cb_ref, wdn_ref, wproj_ref, wgate_ref, g_ref, b_ref,
              stage_ref, carry_ref, act_ref, *, alpha):
    tl = x.shape[0]
    d_ff = wdn_ref.shape[0]
    ck = COL_CHUNK
    xb = x.astype(BF16)

    def conv_cols(col0, slot):
        cols = slice(col0, col0 + ck)
        h = _dot(xb, wup_ref[:, cols])
        return _short_conv(h, stage_ref.at[slot], carry_ref, cols, cw_ref[:, cols]) + cb_ref[:, cols]

    for c in range(d_ff // ck):
        ha = conv_cols(c * ck, 0)
        hg = conv_cols(d_ff + c * ck, 1)
        act_ref[:, c * ck:(c + 1) * ck] = (_silu(ha) * hg).astype(BF16)

    ffn = _dot(act_ref[...], wdn_ref[...])
    gate = _sigmoid(_dot(xb, wgate_ref[...]))
    pe = _load_sublane_major(p_refs, (0, 0), tl // PERM_TILE)
    ple = _dot(pe.astype(BF16), wproj_ref[...]) * gate
    return _layer_norm(alpha * x + ffn + ple, g_ref[...], b_ref[...])


def _ffn_scratch(tl, d_ff, hist):
    return [
        pltpu.VMEM((2, tl // PERM_TILE, hist + S_GROUPS, SUBLANES, COL_CHUNK), F32),
        pltpu.VMEM((hist, SUBLANES, 2 * d_ff), F32),
        pltpu.VMEM((tl, d_ff), BF16),
    ]


N_FFN_CONSTS = 8


def _ffn_kernel(*refs, cast_blocked, alpha):
    ins, cast_src, o_ref, cast_dst, scratch = _split_refs(refs, 3 + N_FFN_CONSTS, cast_blocked)
    x_ref, p0_ref, p1_ref = ins[:3]
    ffn_consts = ins[3:]
    carry_ref = scratch[1]
    unperm = scratch[3:]

    @pl.when(pl.program_id(1) == 0)
    def _():
        carry_ref[...] = jnp.zeros_like(carry_ref)

    _run_casts(cast_blocked, cast_src, cast_dst)
    x = x_ref[0]
    tl, d = x.shape
    y = _ffn_body(x, (p0_ref, p1_ref), *ffn_consts, *scratch[:3], alpha=alpha)
    if not unperm:
        o_ref[0] = y
    else:
        nat_ref, = unperm
        y3 = y.reshape(tl // SUBLANES, SUBLANES, d)
        for u in range(tl // PERM_TILE):
            for i in range(S_GROUPS):
                rows = pl.ds(u * PERM_TILE + i, SUBLANES, stride=S_GROUPS)
                for kb in range(d // LANES):
                    nat_ref[kb, rows, :] = y3[u * S_GROUPS + i, :, kb * LANES:(kb + 1) * LANES]
        for kb in range(d // LANES):
            o_ref[0, :, kb * LANES:(kb + 1) * LANES] = nat_ref[kb]


def _ffn_call(x, p, layer, ffn_consts, cast_jobs, *, alpha, tl, natural_out):
    bsz, seq, d = x.shape
    assert len(ffn_consts) == N_FFN_CONSTS
    d_ff = ffn_consts[3].shape[0]
    hist = ffn_consts[1].shape[0] - 1
    assert p.shape[-1] == 2 * LANES and tl % PERM_TILE == 0
    scratch = _ffn_scratch(tl, d_ff, hist)
    if natural_out:
        scratch.append(pltpu.VMEM((d // LANES, tl, LANES), F32))
    cast_in, cast_out, cast_shapes = _cast_plan(cast_jobs, bsz, seq // tl)
    out = pl.pallas_call(
        functools.partial(_ffn_kernel, cast_blocked=tuple(j[2] for j in cast_jobs), alpha=alpha),
        grid=(bsz, seq // tl),
        in_specs=[
            pl.BlockSpec((1, tl, d), lambda b, t: (b, t, 0)),
            pl.BlockSpec((1, 1, tl, LANES), lambda b, t: (layer, b, t, 0)),
            pl.BlockSpec((1, 1, tl, LANES), lambda b, t: (layer, b, t, 1)),
        ] + [_const_spec(c.shape) for c in ffn_consts] + cast_in,
        out_specs=[pl.BlockSpec((1, tl, d), lambda b, t: (b, t, 0))] + cast_out,
        out_shape=[jax.ShapeDtypeStruct(x.shape, F32)] + cast_shapes,
        scratch_shapes=scratch,
        compiler_params=pltpu.CompilerParams(
            dimension_semantics=("arbitrary", "arbitrary"), vmem_limit_bytes=VMEM_LIMIT),
        name="conv_ffn",
    )(x, p, p, *ffn_consts, *[j[0] for j in cast_jobs])
    return out[0], tuple(out[1:])


def _odd_kernel(*refs, cast_blocked, alpha):
    ins, cast_src, o_ref, cast_dst, scratch = _split_refs(refs, 6, cast_blocked)
    x_ref, win_ref, cw_ref, wout_ref, g_ref, b_ref = ins
    stage_ref, carry_ref, gated_ref = scratch

    @pl.when(pl.program_id(1) == 0)
    def _():
        carry_ref[...] = jnp.zeros_like(carry_ref)

    _run_casts(cast_blocked, cast_src, cast_dst)
    x = x_ref[0]
    xb = x.astype(BF16)
    d_c = wout_ref.shape[0]
    ck = COL_CHUNK
    for c in range(d_c // ck):
        cols = slice(c * ck, (c + 1) * ck)
        bg = _dot(xb, win_ref[:, c * ck:(c + 1) * ck])
        cg = _dot(xb, win_ref[:, d_c + c * ck:d_c + (c + 1) * ck])
        v = _dot(xb, win_ref[:, 2 * d_c + c * ck:2 * d_c + (c + 1) * ck])
        conv = _short_conv(cg * v, stage_ref, carry_ref, cols, cw_ref[:, cols])
        gated_ref[:, cols] = (bg * conv).astype(BF16)
    mix = _dot(gated_ref[...], wout_ref[...])
    o_ref[0] = _layer_norm(alpha * x + mix, g_ref[...], b_ref[...])


def _odd_call(x, w_in, conv_w, w_out, ln_g, ln_b, cast_jobs, *, alpha, tl):
    bsz, seq, d = x.shape
    d_c = w_out.shape[0]
    assert tl % PERM_TILE == 0
    hist = conv_w.shape[0] - 1
    cast_in, cast_out, cast_shapes = _cast_plan(cast_jobs, bsz, seq // tl)
    out = pl.pallas_call(
        functools.partial(_odd_kernel, cast_blocked=tuple(j[2] for j in cast_jobs), alpha=alpha),
        grid=(bsz, seq // tl),
        in_specs=[
            pl.BlockSpec((1, tl, d), lambda b, t: (b, t, 0)),
            _const_spec(w_in.shape), _const_spec(conv_w.shape), _const_spec(w_out.shape),
            _const_spec(ln_g.shape), _const_spec(ln_b.shape),
        ] + cast_in,
        out_specs=[pl.BlockSpec((1, tl, d), lambda b, t: (b, t, 0))] + cast_out,
        out_shape=[jax.ShapeDtypeStruct(x.shape, F32)] + cast_shapes,
        scratch_shapes=[
            pltpu.VMEM((tl // PERM_TILE, hist + S_GROUPS, SUBLANES, COL_CHUNK), F32),
            pltpu.VMEM((hist, SUBLANES, d_c), F32),
            pltpu.VMEM((tl, d_c), BF16),
        ],
        compiler_params=pltpu.CompilerParams(
            dimension_semantics=("arbitrary", "arbitrary"), vmem_limit_bytes=VMEM_LIMIT),
        name="short_conv_mixer",
    )(x, w_in, conv_w, w_out, ln_g, ln_b, *[j[0] for j in cast_jobs])
    return out[0], tuple(out[1:])


def _proj_schedule(n_chunks, n_blocks, slots_per_chunk):
    rest = list(range(3 * n_chunks, n_blocks)) + list(range(2 * n_chunks, 3 * n_chunks))
    order = [0, n_chunks]
    for c in range(n_chunks):
        step = [c + 1, n_chunks + c + 1] if c + 1 < n_chunks else []
        while len(step) < slots_per_chunk and rest:
            step.append(rest.pop(0))
        assert len(step) == slots_per_chunk
        order += step
    return order + rest


def _split3(v):
    hi = v.astype(BF16)
    r1 = v - hi.astype(F32)
    mid = r1.astype(BF16)
    lo = (r1 - mid.astype(F32)).astype(BF16)
    return hi, mid, lo


def _mixer_body(x, win_ref, wdt_ref, caw_ref, cab_ref, lag_ref, lab_ref, cbw_ref, cbb_ref,
                dtb_ref, alog_ref, dskip_ref, nbg_ref, expand_ref, wout_ref, g_ref, b_ref,
                astage_ref, acarry_ref, bstage_ref, bcarry_ref, ca_ref, xbc_ref, ssd_ref,
                state_ref, xb_ref, proj_ref, *, alpha):
    tl = x.shape[0]
    n_blk = tl // PERM_TILE
    d_a = caw_ref.shape[1]
    d_b = nbg_ref.shape[1]
    ck = COL_CHUNK
    ka = caw_ref.shape[0]
    ahist = ka - 1
    gn = N_GROUPS * N_STATE
    n_chunks = d_a // ck
    z_blk = 2 * n_chunks
    xbc_blk = z_blk + d_b // ck
    lane_blocks = ck // LANES
    row_blocks = S_GROUPS // CONV_ROW_BLOCK
    conv_steps = n_blk * row_blocks

    xb_ref[...] = x.astype(BF16)
    n_proj = proj_ref.shape[0]
    order = _proj_schedule(n_chunks, n_proj, PROJ_PER_CHUNK)
    n_ahead = 2 + n_chunks * PROJ_PER_CHUNK

    def projection(blk):
        return _dot(xb_ref[...], win_ref[blk])

    def project(pos):
        proj_ref[order[pos]] = projection(order[pos])

    project(0)
    project(1)

    def stage_chunk(c):
        cols = slice(c * ck, (c + 1) * ck)
        a3 = (proj_ref[c] * _sigmoid(proj_ref[n_chunks + c])).reshape(tl // SUBLANES, SUBLANES, ck)
        stage = astage_ref.at[c % astage_ref.shape[0]]
        carry = acarry_ref[:, :, cols]
        for u in range(n_blk):
            carry = _stage_history(a3[u * S_GROUPS:(u + 1) * S_GROUPS], stage.at[u], carry, ahist)
        acarry_ref[:, :, cols] = carry

    def conv_chunk(c):
        stage = astage_ref.at[c % astage_ref.shape[0]]
        for it in range(conv_steps):
            u = it // row_blocks
            r0 = (it % row_blocks) * CONV_ROW_BLOCK
            for kb in range(lane_blocks):
                ls = slice(kb * LANES, (kb + 1) * LANES)
                gl = slice(c * ck + kb * LANES, c * ck + (kb + 1) * LANES)
                acc = stage[u, ahist + r0:ahist + r0 + CONV_ROW_BLOCK, :, ls] * caw_ref[ka - 1:ka, gl] + cab_ref[:, gl]
                for j in range(ka - 1):
                    acc = acc + stage[u, j + r0:j + r0 + CONV_ROW_BLOCK, :, ls] * caw_ref[j:j + 1, gl]
                g0 = u * S_GROUPS + r0
                ca_ref[g0:g0 + CONV_ROW_BLOCK, :, gl] = acc
        for k in range(PROJ_PER_CHUNK):
            project(2 + c * PROJ_PER_CHUNK + k)

    for c in range(n_chunks):
        stage_chunk(c)
        pl.when(pl.program_id(0) >= 0)(functools.partial(conv_chunk, c))
    late = {order[pos]: projection(order[pos]) for pos in range(n_ahead, n_proj)}
    ya = _silu(_layer_norm(ca_ref[...].reshape(tl, d_a), lag_ref[...], lab_ref[...]))
    mix_a = _dot(ya.astype(BF16), wout_ref[0:d_a, :])

    for c in range((d_b + 2 * gn) // ck):
        cols = slice(c * ck, (c + 1) * ck)
        h = proj_ref[xbc_blk + c]
        conv = _short_conv(h, bstage_ref, bcarry_ref, cols, cbw_ref[:, cols]) + cbb_ref[:, cols]
        xbc_ref[:, cols] = _silu(conv)

    dt_all = _softplus(_dot(xb_ref[...], wdt_ref[...]) + dtb_ref[...])
    da_all = dt_all * (-LOG2E * jnp.exp(alog_ref[...]))

    def token_of(r):
        return S_GROUPS * (r % SUBLANES) + r // SUBLANES

    q = PERM_TILE
    causal = (token_of(lax.broadcasted_iota(jnp.int32, (q, q), 0))
              >= token_of(lax.broadcasted_iota(jnp.int32, (q, q), 1)))
    tril = jnp.where(causal, 1.0, 0.0).astype(BF16)
    lane = lax.broadcasted_iota(jnp.int32, (q, LANES), 1)
    half = [lane < HEAD_DIM, lane >= HEAD_DIM]
    pairs_per_group = (d_b // LANES) // N_GROUPS

    for u in range(n_blk):
        rows = slice(u * q, (u + 1) * q)
        dt = dt_all[rows]
        hi, mid, lo = _split3(da_all[rows])
        cum = _dot(tril, hi) + _dot(tril, mid) + _dot(tril, lo)
        src_t = (cum - jnp.log(dt) * LOG2E).T
        last = cum[q - 1:q, :]
        dstate = jnp.exp2(last - cum) * dt
        pad = jnp.zeros((SUBLANES - 1, LANES), F32)
        scal = jnp.concatenate([dstate, jnp.exp2(last), pad], axis=0)
        s_hi = scal.astype(BF16)
        s_lo = (scal - s_hi.astype(F32)).astype(BF16)
        expd = _dot(jnp.concatenate([s_hi, s_lo], axis=1), expand_ref[...])
        dstate_x = expd[0:q]
        cdecay_x = expd[q:q + 1]
        st_b = state_ref[...].astype(BF16)

        for g in range(N_GROUPS):
            bm = xbc_ref[rows, d_b + g * N_STATE:d_b + (g + 1) * N_STATE].astype(BF16)
            cm = xbc_ref[rows, d_b + gn + g * N_STATE:d_b + gn + (g + 1) * N_STATE]
            cb = lax.dot_general(cm.astype(BF16), bm, (((1,), (1,)), ((), ())),
                                 preferred_element_type=F32)
            scaled = []
            for pr in range(g * pairs_per_group, (g + 1) * pairs_per_group):
                lanes = slice(pr * LANES, (pr + 1) * LANES)
                xs = xbc_ref[rows, lanes]
                xp = xs.astype(BF16)
                sp = st_b[:, lanes]
                lhs, rhs = [], []
                for j in range(2):
                    hd = 2 * pr + j
                    ccol = jnp.broadcast_to(cum[:, hd:hd + 1], (q, q))
                    srow = jnp.broadcast_to(src_t[hd:hd + 1, :], (q, q))
                    lhs.append((cb * jnp.where(causal, jnp.exp2(ccol - srow), 0.0)).astype(BF16))
                    lhs.append((cm * jnp.exp2(ccol)).astype(BF16))
                    rhs.append(jnp.where(half[j], xp, jnp.zeros_like(xp)))
                    rhs.append(jnp.where(half[j], sp, jnp.zeros_like(sp)))
                y = _dot(jnp.concatenate(lhs, axis=1), jnp.concatenate(rhs, axis=0))
                ssd_ref[rows, lanes] = y + dskip_ref[:, lanes] * xs
                scaled.append((xs * dstate_x[:, lanes]).astype(BF16))
            gcols = slice(g * pairs_per_group * LANES, (g + 1) * pairs_per_group * LANES)
            upd = lax.dot_general(bm, jnp.concatenate(scaled, axis=1), (((0,), (0,)), ((), ())),
                                  preferred_element_type=F32)
            state_ref[:, gcols] = state_ref[:, gcols] * cdecay_x[:, gcols] + upd

    z = jnp.concatenate([late[blk] if blk in late else proj_ref[blk]
                         for blk in range(z_blk, z_blk + d_b // ck)], axis=1)
    y = ssd_ref[...] * _silu(z)
    yb = y * lax.rsqrt(jnp.mean(y * y, axis=-1, keepdims=True) + LN_EPS) * nbg_ref[...]
    mix = mix_a + _dot(yb.astype(BF16), wout_ref[d_a:d_a + d_b, :])
    return _layer_norm(alpha * x + mix, g_ref[...], b_ref[...])


N_MIXER_CONSTS = 16


def _even_kernel(*refs, n_x, cast_blocked, alpha):
    ins, cast_src, o_ref, cast_dst, scratch = _split_refs(refs, n_x + N_MIXER_CONSTS, cast_blocked)
    x_refs = ins[:n_x]
    mixer_consts = ins[n_x:]
    acarry_ref, bcarry_ref, state_ref = scratch[1], scratch[3], scratch[7]
    tl = o_ref.shape[1]

    @pl.when(pl.program_id(1) == 0)
    def _():
        for r in (acarry_ref, bcarry_ref, state_ref):
            r[...] = jnp.zeros_like(r)

    _run_casts(cast_blocked, cast_src, cast_dst)
    if n_x == 1:
        x = x_refs[0][0]
    else:
        x = _load_sublane_major(x_refs, (0,), tl // PERM_TILE)
    o_ref[0] = _mixer_body(x, *mixer_consts, *scratch, alpha=alpha)


def _even_call(x, mixer_consts, cast_jobs, *, alpha, tl, natural_in):
    bsz, seq, d = x.shape
    assert len(mixer_consts) == N_MIXER_CONSTS
    w_main, conv_a_w, conv_b_w, norm_b_g = mixer_consts[0], mixer_consts[2], mixer_consts[6], mixer_consts[11]
    d_a = conv_a_w.shape[1]
    d_b = norm_b_g.shape[1]
    xbc_dim = conv_b_w.shape[1]
    ck = COL_CHUNK
    ahist = conv_a_w.shape[0] - 1
    bhist = conv_b_w.shape[0] - 1
    n_blk = tl // PERM_TILE
    assert tl % PERM_TILE == 0 and PERM_TILE == N_STATE and d_a == d_b
    n_proj = w_main.shape[0]
    assert w_main.shape[1:] == (d, ck)
    cast_in, cast_out, cast_shapes = _cast_plan(cast_jobs, bsz, seq // tl)

    if natural_in:
        n_x = d // LANES
        x_specs = [pl.BlockSpec((1, tl, LANES), functools.partial(lambda b, t, kb: (b, t, kb), kb=kb))
                   for kb in range(n_x)]
    else:
        n_x = 1
        x_specs = [pl.BlockSpec((1, tl, d), lambda b, t: (b, t, 0))]
    mixer_scratch = [
        pltpu.VMEM((2, n_blk, ahist + S_GROUPS, SUBLANES, ck), F32),
        pltpu.VMEM((ahist, SUBLANES, d_a), F32),
        pltpu.VMEM((n_blk, bhist + S_GROUPS, SUBLANES, ck), F32),
        pltpu.VMEM((bhist, SUBLANES, xbc_dim), F32),
        pltpu.VMEM((tl // SUBLANES, SUBLANES, d_a), F32),
        pltpu.VMEM((tl, xbc_dim), F32),
        pltpu.VMEM((tl, d_b), F32),
        pltpu.VMEM((N_STATE, d_b), F32),
        pltpu.VMEM((tl, d), BF16),
        pltpu.VMEM((n_proj, tl, ck), F32),
    ]
    out = pl.pallas_call(
        functools.partial(_even_kernel, n_x=n_x, cast_blocked=tuple(j[2] for j in cast_jobs), alpha=alpha),
        grid=(bsz, seq // tl),
        in_specs=x_specs + [_const_spec(c.shape) for c in mixer_consts] + cast_in,
        out_specs=[pl.BlockSpec((1, tl, d), lambda b, t: (b, t, 0))] + cast_out,
        out_shape=[jax.ShapeDtypeStruct(x.shape, F32)] + cast_shapes,
        scratch_shapes=mixer_scratch,
        compiler_params=pltpu.CompilerParams(
            dimension_semantics=("arbitrary", "arbitrary"), vmem_limit_bytes=VMEM_LIMIT),
        name="conformer_ssd_mixer",
    )(*([x] * n_x), *mixer_consts, *[j[0] for j in cast_jobs])
    return out[0], tuple(out[1:])


def _row(v):
    return v.reshape(1, -1).astype(F32)


def _pad_lanes(v):
    return jnp.pad(v, ((0, 0), (0, LANES - v.shape[1])))


def kernel(x, p, e_w_in, e_conv_a_w, e_conv_a_b, e_ln_a_g, e_ln_a_b, e_conv_b_w, e_conv_b_b, e_dt_bias, e_a_log, e_d_skip, e_norm_b_g, e_w_out, o_w_in, o_conv_w, o_w_out, f_w_up, f_conv_w, f_conv_b, f_w_down, ple_w_proj, ple_w_gate, ln_g, ln_b):
    depth = f_w_up.shape[0]
    alpha = (2.0 * depth) ** 0.25
    d_b = e_norm_b_g.shape[1]
    n_heads = e_dt_bias.shape[1]
    o_dt = e_w_in.shape[2] - n_heads
    tl_even = tl_odd = tl_ffn = min(512, x.shape[1])

    head_of_col = jnp.arange(d_b) // HEAD_DIM
    sel = (jnp.arange(LANES)[:, None] == head_of_col[None, :]).astype(BF16)
    expand = jnp.concatenate([sel, sel], axis=0)

    def mixer_jobs(i):
        if i % 2 == 0:
            return [(e_w_in, i // 2, True), (e_w_out, i // 2, False)]
        return [(o_w_in, i // 2, False), (o_w_out, i // 2, False)]

    def ffn_jobs(i):
        return [(f_w_up, i, False), (f_w_down, i, False), (ple_w_proj, i, False), (ple_w_gate, i, False)]

    weights = _blocked_bf16(e_w_in[0]) + (e_w_out[0].astype(BF16),)
    for i in range(depth):
        j = i // 2
        if i % 2 == 0:
            w_main, w_dt, w_out = weights
            mixer_consts = (
                w_main, w_dt,
                e_conv_a_w[j], _row(e_conv_a_b[j]), _row(e_ln_a_g[j]), _row(e_ln_a_b[j]),
                e_conv_b_w[j], _row(e_conv_b_b[j]),
                _pad_lanes(_row(e_dt_bias[j])), _pad_lanes(_row(e_a_log[j])),
                _row(jnp.repeat(e_d_skip[j], HEAD_DIM)), _row(e_norm_b_g[j]), expand,
                w_out, _row(ln_g[i, 0]), _row(ln_b[i, 0]))
            x, weights = _even_call(x, mixer_consts, ffn_jobs(i), alpha=alpha, tl=tl_even, natural_in=(i == 0))
        else:
            w_in, w_out = weights
            x, weights = _odd_call(x, w_in, o_conv_w[j], w_out, _row(ln_g[i, 0]), _row(ln_b[i, 0]),
                                   ffn_jobs(i), alpha=alpha, tl=tl_odd)
        w_up, w_down, w_proj, w_gate = weights
        ffn_consts = (w_up, f_conv_w[i], _row(f_conv_b[i]), w_down, w_proj, w_gate,
                      _row(ln_g[i, 1]), _row(ln_b[i, 1]))
        x, weights = _ffn_call(x, p, i, ffn_consts, mixer_jobs(i + 1) if i + 1 < depth else [],
                               alpha=alpha, tl=tl_ffn, natural_out=(i == depth - 1))
    return x
```

```python
import functools

import jax
import jax.numpy as jnp
from jax import lax
from jax.experimental import pallas as pl
from jax.experimental.pallas import tpu as pltpu

F32 = jnp.float32
BF16 = jnp.bfloat16

LN_EPS = 1e-5
LOG2E = 1.4426950408889634
HEAD_DIM = 64
N_GROUPS = 4
N_STATE = 128
LANES = 128
SUBLANES = 8
PERM_TILE = 128
S_GROUPS = PERM_TILE // SUBLANES
CONV_ROW_BLOCK = 16
PROJ_PER_CHUNK = 4
COL_CHUNK = 256
VMEM_LIMIT = 60 * 1024 * 1024


def _dot(a, b):
    return jnp.dot(a, b, preferred_element_type=F32)


def _layer_norm(v, g, b):
    mu = jnp.mean(v, axis=-1, keepdims=True)
    d = v - mu
    var = jnp.mean(d * d, axis=-1, keepdims=True)
    return d * lax.rsqrt(var + LN_EPS) * g + b


def _sigmoid(v):
    return 1.0 / (1.0 + jnp.exp(-v))


def _silu(v):
    return v * _sigmoid(v)


def _softplus(v):
    return jnp.maximum(v, 0.0) + jnp.log(1.0 + jnp.exp(-jnp.abs(v)))


def _load_sublane_major(refs, lead, n_blocks):
    groups = []
    for u in range(n_blocks):
        for i in range(S_GROUPS):
            rows = pl.ds(u * PERM_TILE + i, SUBLANES, stride=S_GROUPS)
            groups.append(jnp.concatenate([r[lead + (rows, slice(None))] for r in refs], axis=1))
    return jnp.concatenate(groups, axis=0)


def _stage_history(h3, stage_ref, carry, hist):
    n1 = min(hist, S_GROUPS)
    n2 = hist - n1
    assert n2 <= S_GROUPS
    rolled = pltpu.roll(h3[S_GROUPS - n1:], 1, axis=1)
    if n2:
        rolled = jnp.concatenate([pltpu.roll(h3[S_GROUPS - n2:], 2, axis=1), rolled], axis=0)
    sub = lax.broadcasted_iota(jnp.int32, rolled.shape, 1)
    grp = lax.broadcasted_iota(jnp.int32, rolled.shape, 0)
    from_prev = sub < jnp.where(grp < n2, 2, 1)
    stage_ref[0:hist] = jnp.where(from_prev, carry, rolled)
    stage_ref[hist:hist + S_GROUPS] = h3
    return rolled


def _short_conv(h, stage_ref, carry_ref, cols, w):
    tl, n = h.shape
    k = w.shape[0]
    hist = k - 1
    h3 = h.reshape(tl // SUBLANES, SUBLANES, n)
    carry = carry_ref[:, :, cols]
    outs = []
    for u in range(tl // PERM_TILE):
        hu = h3[u * S_GROUPS:(u + 1) * S_GROUPS]
        carry = _stage_history(hu, stage_ref.at[u], carry, hist)
        out = hu * w[k - 1:k, :]
        for j in range(k - 1):
            out = out + stage_ref[u, j:j + S_GROUPS] * w[j:j + 1, :]
        outs.append(out)
    carry_ref[:, :, cols] = carry
    return jnp.concatenate(outs, axis=0).reshape(tl, n)


def _const_spec(shape):
    nd = len(shape)
    return pl.BlockSpec(shape, lambda b, t: (0,) * nd, pipeline_mode=pl.Buffered(1))


BF16_SUBLANES = 16


def _cast_plan(jobs, n_b, n_t):
    steps = n_b * n_t
    in_specs, out_specs, out_shapes = [], [], []
    for w, layer, blocked in jobs:
        _, r, c = w.shape
        slabs = max(k for k in range(1, steps + 1)
                    if steps % k == 0 and r % k == 0 and (r // k) % BF16_SUBLANES == 0)
        per = steps // slabs
        rows = r // slabs
        in_specs.append(pl.BlockSpec(
            (1, rows, c), functools.partial(lambda b, t, layer, per: (layer, (b * n_t + t) // per, 0),
                                            layer=layer, per=per)))
        row_slab = functools.partial(lambda b, t, per: ((b * n_t + t) // per, 0), per=per)
        if not blocked:
            out_specs.append(pl.BlockSpec((rows, c), row_slab))
            out_shapes.append(jax.ShapeDtypeStruct((r, c), BF16))
        else:
            nb = c // COL_CHUNK
            assert 0 < c - nb * COL_CHUNK <= LANES
            out_specs.append(pl.BlockSpec(
                (nb, rows, COL_CHUNK), functools.partial(lambda b, t, per: (0, (b * n_t + t) // per, 0), per=per)))
            out_shapes.append(jax.ShapeDtypeStruct((nb, r, COL_CHUNK), BF16))
            out_specs.append(pl.BlockSpec((rows, LANES), row_slab))
            out_shapes.append(jax.ShapeDtypeStruct((r, LANES), BF16))
    return in_specs, out_specs, out_shapes


def _cast_kernel(*refs, cast_blocked):
    n_src = len(cast_blocked)
    _run_casts(cast_blocked, refs[:n_src], refs[n_src:])


def _cast_call(jobs, n_b, n_t):
    cast_in, cast_out, cast_shapes = _cast_plan(jobs, n_b, n_t)
    return tuple(pl.pallas_call(
        functools.partial(_cast_kernel, cast_blocked=tuple(j[2] for j in jobs)),
        grid=(n_b, n_t),
        in_specs=cast_in,
        out_specs=cast_out,
        out_shape=cast_shapes,
        compiler_params=pltpu.CompilerParams(dimension_semantics=("arbitrary", "arbitrary")),
        name="weight_cast",
    )(*[j[0] for j in jobs]))


def _split_refs(refs, n_in, blocked):
    n_src = len(blocked)
    n_dst = sum(2 if b else 1 for b in blocked)
    k = n_in + n_src
    return refs[:n_in], refs[n_in:k], refs[k], refs[k + 1:k + 1 + n_dst], refs[k + 1 + n_dst:]


def _run_casts(blocked_flags, src_refs, dst_refs):
    dst = list(dst_refs)
    for blocked, src in zip(blocked_flags, src_refs):
        w = src[0].astype(BF16)
        if not blocked:
            dst.pop(0)[...] = w
        else:
            main, rest = dst.pop(0), dst.pop(0)
            nb = main.shape[0]
            for blk in range(nb):
                main[blk] = w[:, blk * COL_CHUNK:(blk + 1) * COL_CHUNK]
            tail = w[:, nb * COL_CHUNK:]
            rest[...] = jnp.concatenate(
                [tail, jnp.zeros((tail.shape[0], LANES - tail.shape[1]), BF16)], axis=1)


def _ffn_body(x, p_refs, wup_ref, cw_ref, cb_ref, wdn_ref, wproj_ref, wgate_ref, g_ref, b_ref,
              stage_ref, carry_ref, act_ref, *, alpha):
    tl = x.shape[0]
    d_ff = wdn_ref.shape[0]
    ck = COL_CHUNK
    xb = x.astype(BF16)

    def conv_cols(col0, slot):
        cols = slice(col0, col0 + ck)
        h = _dot(xb, wup_ref[:, cols])
        return _short_conv(h, stage_ref.at[slot], carry_ref, cols, cw_ref[:, cols]) + cb_ref[:, cols]

    for c in range(d_ff // ck):
        ha = conv_cols(c * ck, 0)
        hg = conv_cols(d_ff + c * ck, 1)
        act_ref[:, c * ck:(c + 1) * ck] = (_silu(ha) * hg).astype(BF16)

    ffn = _dot(act_ref[...], wdn_ref[...])
    gate = _sigmoid(_dot(xb, wgate_ref[...]))
    pe = _load_sublane_major(p_refs, (0, 0), tl // PERM_TILE)
    ple = _dot(pe.astype(BF16), wproj_ref[...]) * gate
    return _layer_norm(alpha * x + ffn + ple, g_ref[...], b_ref[...])


def _ffn_scratch(tl, d_ff, hist):
    return [
        pltpu.VMEM((2, tl // PERM_TILE, hist + S_GROUPS, SUBLANES, COL_CHUNK), F32),
        pltpu.VMEM((hist, SUBLANES, 2 * d_ff), F32),
        pltpu.VMEM((tl, d_ff), BF16),
    ]


N_FFN_CONSTS = 8


def _ffn_kernel(*refs, cast_blocked, alpha):
    ins, cast_src, o_ref, cast_dst, scratch = _split_refs(refs, 3 + N_FFN_CONSTS, cast_blocked)
    x_ref, p0_ref, p1_ref = ins[:3]
    ffn_consts = ins[3:]
    carry_ref = scratch[1]
    unperm = scratch[3:]

    @pl.when(pl.program_id(1) == 0)
    def _():
        carry_ref[...] = jnp.zeros_like(carry_ref)

    _run_casts(cast_blocked, cast_src, cast_dst)
    x = x_ref[0]
    tl, d = x.shape
    y = _ffn_body(x, (p0_ref, p1_ref), *ffn_consts, *scratch[:3], alpha=alpha)
    if not unperm:
        o_ref[0] = y
    else:
        nat_ref, = unperm
        y3 = y.reshape(tl // SUBLANES, SUBLANES, d)
        for u in range(tl // PERM_TILE):
            for i in range(S_GROUPS):
                rows = pl.ds(u * PERM_TILE + i, SUBLANES, stride=S_GROUPS)
                for kb in range(d // LANES):
                    nat_ref[kb, rows, :] = y3[u * S_GROUPS + i, :, kb * LANES:(kb + 1) * LANES]
        for kb in range(d // LANES):
            o_ref[0, :, kb * LANES:(kb + 1) * LANES] = nat_ref[kb]


def _ffn_call(x, p, layer, ffn_consts, cast_jobs, *, alpha, tl, natural_out):
    bsz, seq, d = x.shape
    assert len(ffn_consts) == N_FFN_CONSTS
    d_ff = ffn_consts[3].shape[0]
    hist = ffn_consts[1].shape[0] - 1
    assert p.shape[-1] == 2 * LANES and tl % PERM_TILE == 0
    scratch = _ffn_scratch(tl, d_ff, hist)
    if natural_out:
        scratch.append(pltpu.VMEM((d // LANES, tl, LANES), F32))
    cast_in, cast_out, cast_shapes = _cast_plan(cast_jobs, bsz, seq // tl)
    out = pl.pallas_call(
        functools.partial(_ffn_kernel, cast_blocked=tuple(j[2] for j in cast_jobs), alpha=alpha),
        grid=(bsz, seq // tl),
        in_specs=[
            pl.BlockSpec((1, tl, d), lambda b, t: (b, t, 0)),
            pl.BlockSpec((1, 1, tl, LANES), lambda b, t: (layer, b, t, 0)),
            pl.BlockSpec((1, 1, tl, LANES), lambda b, t: (layer, b, t, 1)),
        ] + [_const_spec(c.shape) for c in ffn_consts] + cast_in,
        out_specs=[pl.BlockSpec((1, tl, d), lambda b, t: (b, t, 0))] + cast_out,
        out_shape=[jax.ShapeDtypeStruct(x.shape, F32)] + cast_shapes,
        scratch_shapes=scratch,
        compiler_params=pltpu.CompilerParams(
            dimension_semantics=("arbitrary", "arbitrary"), vmem_limit_bytes=VMEM_LIMIT),
        name="conv_ffn",
    )(x, p, p, *ffn_consts, *[j[0] for j in cast_jobs])
    return out[0], tuple(out[1:])


def _odd_kernel(*refs, cast_blocked, alpha):
    ins, cast_src, o_ref, cast_dst, scratch = _split_refs(refs, 6, cast_blocked)
    x_ref, win_ref, cw_ref, wout_ref, g_ref, b_ref = ins
    stage_ref, carry_ref, gated_ref = scratch

    @pl.when(pl.program_id(1) == 0)
    def _():
        carry_ref[...] = jnp.zeros_like(carry_ref)

    _run_casts(cast_blocked, cast_src, cast_dst)
    x = x_ref[0]
    xb = x.astype(BF16)
    d_c = wout_ref.shape[0]
    ck = COL_CHUNK
    for c in range(d_c // ck):
        cols = slice(c * ck, (c + 1) * ck)
        bg = _dot(xb, win_ref[:, c * ck:(c + 1) * ck])
        cg = _dot(xb, win_ref[:, d_c + c * ck:d_c + (c + 1) * ck])
        v = _dot(xb, win_ref[:, 2 * d_c + c * ck:2 * d_c + (c + 1) * ck])
        conv = _short_conv(cg * v, stage_ref, carry_ref, cols, cw_ref[:, cols])
        gated_ref[:, cols] = (bg * conv).astype(BF16)
    mix = _dot(gated_ref[...], wout_ref[...])
    o_ref[0] = _layer_norm(alpha * x + mix, g_ref[...], b_ref[...])


def _odd_call(x, w_in, conv_w, w_out, ln_g, ln_b, cast_jobs, *, alpha, tl):
    bsz, seq, d = x.shape
    d_c = w_out.shape[0]
    assert tl % PERM_TILE == 0
    hist = conv_w.shape[0] - 1
    cast_in, cast_out, cast_shapes = _cast_plan(cast_jobs, bsz, seq // tl)
    out = pl.pallas_call(
        functools.partial(_odd_kernel, cast_blocked=tuple(j[2] for j in cast_jobs), alpha=alpha),
        grid=(bsz, seq // tl),
        in_specs=[
            pl.BlockSpec((1, tl, d), lambda b, t: (b, t, 0)),
            _const_spec(w_in.shape), _const_spec(conv_w.shape), _const_spec(w_out.shape),
            _const_spec(ln_g.shape), _const_spec(ln_b.shape),
        ] + cast_in,
        out_specs=[pl.BlockSpec((1, tl, d), lambda b, t: (b, t, 0))] + cast_out,
        out_shape=[jax.ShapeDtypeStruct(x.shape, F32)] + cast_shapes,
        scratch_shapes=[
            pltpu.VMEM((tl // PERM_TILE, hist + S_GROUPS, SUBLANES, COL_CHUNK), F32),
            pltpu.VMEM((hist, SUBLANES, d_c), F32),
            pltpu.VMEM((tl, d_c), BF16),
        ],
        compiler_params=pltpu.CompilerParams(
            dimension_semantics=("arbitrary", "arbitrary"), vmem_limit_bytes=VMEM_LIMIT),
        name="short_conv_mixer",
    )(x, w_in, conv_w, w_out, ln_g, ln_b, *[j[0] for j in cast_jobs])
    return out[0], tuple(out[1:])


def _proj_schedule(n_chunks, n_blocks, slots_per_chunk):
    rest = list(range(3 * n_chunks, n_blocks)) + list(range(2 * n_chunks, 3 * n_chunks))
    order = [0, n_chunks]
    for c in range(n_chunks):
        step = [c + 1, n_chunks + c + 1] if c + 1 < n_chunks else []
        while len(step) < slots_per_chunk and rest:
            step.append(rest.pop(0))
        assert len(step) == slots_per_chunk
        order += step
    return order + rest


def _split3(v):
    hi = v.astype(BF16)
    r1 = v - hi.astype(F32)
    mid = r1.astype(BF16)
    lo = (r1 - mid.astype(F32)).astype(BF16)
    return hi, mid, lo


def _mixer_body(x, win_ref, wdt_ref, caw_ref, cab_ref, lag_ref, lab_ref, cbw_ref, cbb_ref,
                dtb_ref, alog_ref, dskip_ref, nbg_ref, expand_ref, wout_ref, g_ref, b_ref,
                astage_ref, acarry_ref, bstage_ref, bcarry_ref, ca_ref, xbc_ref, ssd_ref,
                state_ref, xb_ref, proj_ref, *, alpha):
    tl = x.shape[0]
    n_blk = tl // PERM_TILE
    d_a = caw_ref.shape[1]
    d_b = nbg_ref.shape[1]
    ck = COL_CHUNK
    ka = caw_ref.shape[0]
    ahist = ka - 1
    gn = N_GROUPS * N_STATE
    n_chunks = d_a // ck
    z_blk = 2 * n_chunks
    xbc_blk = z_blk + d_b // ck
    lane_blocks = ck // LANES
    row_blocks = S_GROUPS // CONV_ROW_BLOCK
    conv_steps = n_blk * row_blocks

    xb_ref[...] = x.astype(BF16)
    n_proj = proj_ref.shape[0]
    order = _proj_schedule(n_chunks, n_proj, PROJ_PER_CHUNK)
    n_ahead = 2 + n_chunks * PROJ_PER_CHUNK

    def projection(blk):
        return _dot(xb_ref[...], win_ref[blk])

    def project(pos):
        proj_ref[order[pos]] = projection(order[pos])

    project(0)
    project(1)

    def stage_chunk(c):
        cols = slice(c * ck, (c + 1) * ck)
        a3 = (proj_ref[c] * _sigmoid(proj_ref[n_chunks + c])).reshape(tl // SUBLANES, SUBLANES, ck)
        stage = astage_ref.at[c % astage_ref.shape[0]]
        carry = acarry_ref[:, :, cols]
        for u in range(n_blk):
            carry = _stage_history(a3[u * S_GROUPS:(u + 1) * S_GROUPS], stage.at[u], carry, ahist)
        acarry_ref[:, :, cols] = carry

    def conv_chunk(c):
        stage = astage_ref.at[c % astage_ref.shape[0]]
        for it in range(conv_steps):
            u = it // row_blocks
            r0 = (it % row_blocks) * CONV_ROW_BLOCK
            for kb in range(lane_blocks):
                ls = slice(kb * LANES, (kb + 1) * LANES)
                gl = slice(c * ck + kb * LANES, c * ck + (kb + 1) * LANES)
                acc = stage[u, ahist + r0:ahist + r0 + CONV_ROW_BLOCK, :, ls] * caw_ref[ka - 1:ka, gl] + cab_ref[:, gl]
                for j in range(ka - 1):
                    acc = acc + stage[u, j + r0:j + r0 + CONV_ROW_BLOCK, :, ls] * caw_ref[j:j + 1, gl]
                g0 = u * S_GROUPS + r0
                ca_ref[g0:g0 + CONV_ROW_BLOCK, :, gl] = acc
        for k in range(PROJ_PER_CHUNK):
            project(2 + c * PROJ_PER_CHUNK + k)

    for c in range(n_chunks):
        stage_chunk(c)
        pl.when(pl.program_id(0) >= 0)(functools.partial(conv_chunk, c))
    late = {order[pos]: projection(order[pos]) for pos in range(n_ahead, n_proj)}
    ya = _silu(_layer_norm(ca_ref[...].reshape(tl, d_a), lag_ref[...], lab_ref[...]))
    mix_a = _dot(ya.astype(BF16), wout_ref[0:d_a, :])

    for c in range((d_b + 2 * gn) // ck):
        cols = slice(c * ck, (c + 1) * ck)
        h = proj_ref[xbc_blk + c]
        conv = _short_conv(h, bstage_ref, bcarry_ref, cols, cbw_ref[:, cols]) + cbb_ref[:, cols]
        xbc_ref[:, cols] = _silu(conv)

    dt_all = _softplus(_dot(xb_ref[...], wdt_ref[...]) + dtb_ref[...])
    da_all = dt_all * (-LOG2E * jnp.exp(alog_ref[...]))

    def token_of(r):
        return S_GROUPS * (r % SUBLANES) + r // SUBLANES

    q = PERM_TILE
    causal = (token_of(lax.broadcasted_iota(jnp.int32, (q, q), 0))
              >= token_of(lax.broadcasted_iota(jnp.int32, (q, q), 1)))
    tril = jnp.where(causal, 1.0, 0.0).astype(BF16)
    lane = lax.broadcasted_iota(jnp.int32, (q, LANES), 1)
    half = [lane < HEAD_DIM, lane >= HEAD_DIM]
    pairs_per_group = (d_b // LANES) // N_GROUPS

    for u in range(n_blk):
        rows = slice(u * q, (u + 1) * q)
        dt = dt_all[rows]
        hi, mid, lo = _split3(da_all[rows])
        cum = _dot(tril, hi) + _dot(tril, mid) + _dot(tril, lo)
        src_t = (cum - jnp.log(dt) * LOG2E).T
        last = cum[q - 1:q, :]
        dstate = jnp.exp2(last - cum) * dt
        pad = jnp.zeros((SUBLANES - 1, LANES), F32)
        scal = jnp.concatenate([dstate, jnp.exp2(last), pad], axis=0)
        s_hi = scal.astype(BF16)
        s_lo = (scal - s_hi.astype(F32)).astype(BF16)
        expd = _dot(jnp.concatenate([s_hi, s_lo], axis=1), expand_ref[...])
        dstate_x = expd[0:q]
        cdecay_x = expd[q:q + 1]
        st_b = state_ref[...].astype(BF16)

        for g in range(N_GROUPS):
            bm = xbc_ref[rows, d_b + g * N_STATE:d_b + (g + 1) * N_STATE].astype(BF16)
            cm = xbc_ref[rows, d_b + gn + g * N_STATE:d_b + gn + (g + 1) * N_STATE]
            cb = lax.dot_general(cm.astype(BF16), bm, (((1,), (1,)), ((), ())),
                                 preferred_element_type=F32)
            scaled = []
            for pr in range(g * pairs_per_group, (g + 1) * pairs_per_group):
                lanes = slice(pr * LANES, (pr + 1) * LANES)
                xs = xbc_ref[rows, lanes]
                xp = xs.astype(BF16)
                sp = st_b[:, lanes]
                lhs, rhs = [], []
                for j in range(2):
                    hd = 2 * pr + j
                    ccol = jnp.broadcast_to(cum[:, hd:hd + 1], (q, q))
                    srow = jnp.broadcast_to(src_t[hd:hd + 1, :], (q, q))
                    lhs.append((cb * jnp.where(causal, jnp.exp2(ccol - srow), 0.0)).astype(BF16))
                    lhs.append((cm * jnp.exp2(ccol)).astype(BF16))
                    rhs.append(jnp.where(half[j], xp, jnp.zeros_like(xp)))
                    rhs.append(jnp.where(half[j], sp, jnp.zeros_like(sp)))
                y = _dot(jnp.concatenate(lhs, axis=1), jnp.concatenate(rhs, axis=0))
                ssd_ref[rows, lanes] = y + dskip_ref[:, lanes] * xs
                scaled.append((xs * dstate_x[:, lanes]).astype(BF16))
            gcols = slice(g * pairs_per_group * LANES, (g + 1) * pairs_per_group * LANES)
            upd = lax.dot_general(bm, jnp.concatenate(scaled, axis=1), (((0,), (0,)), ((), ())),
                                  preferred_element_type=F32)
            state_ref[:, gcols] = state_ref[:, gcols] * cdecay_x[:, gcols] + upd

    z = jnp.concatenate([late[blk] if blk in late else proj_ref[blk]
                         for blk in range(z_blk, z_blk + d_b // ck)], axis=1)
    y = ssd_ref[...] * _silu(z)
    yb = y * lax.rsqrt(jnp.mean(y * y, axis=-1, keepdims=True) + LN_EPS) * nbg_ref[...]
    mix = mix_a + _dot(yb.astype(BF16), wout_ref[d_a:d_a + d_b, :])
    return _layer_norm(alpha * x + mix, g_ref[...], b_ref[...])


N_MIXER_CONSTS = 16


def _even_kernel(*refs, n_x, cast_blocked, alpha):
    ins, cast_src, o_ref, cast_dst, scratch = _split_refs(refs, n_x + N_MIXER_CONSTS, cast_blocked)
    x_refs = ins[:n_x]
    mixer_consts = ins[n_x:]
    acarry_ref, bcarry_ref, state_ref = scratch[1], scratch[3], scratch[7]
    tl = o_ref.shape[1]

    @pl.when(pl.program_id(1) == 0)
    def _():
        for r in (acarry_ref, bcarry_ref, state_ref):
            r[...] = jnp.zeros_like(r)

    _run_casts(cast_blocked, cast_src, cast_dst)
    if n_x == 1:
        x = x_refs[0][0]
    else:
        x = _load_sublane_major(x_refs, (0,), tl // PERM_TILE)
    o_ref[0] = _mixer_body(x, *mixer_consts, *scratch, alpha=alpha)


def _even_call(x, mixer_consts, cast_jobs, *, alpha, tl, natural_in):
    bsz, seq, d = x.shape
    assert len(mixer_consts) == N_MIXER_CONSTS
    w_main, conv_a_w, conv_b_w, norm_b_g = mixer_consts[0], mixer_consts[2], mixer_consts[6], mixer_consts[11]
    d_a = conv_a_w.shape[1]
    d_b = norm_b_g.shape[1]
    xbc_dim = conv_b_w.shape[1]
    ck = COL_CHUNK
    ahist = conv_a_w.shape[0] - 1
    bhist = conv_b_w.shape[0] - 1
    n_blk = tl // PERM_TILE
    assert tl % PERM_TILE == 0 and PERM_TILE == N_STATE and d_a == d_b
    n_proj = w_main.shape[0]
    assert w_main.shape[1:] == (d, ck)
    cast_in, cast_out, cast_shapes = _cast_plan(cast_jobs, bsz, seq // tl)

    if natural_in:
        n_x = d // LANES
        x_specs = [pl.BlockSpec((1, tl, LANES), functools.partial(lambda b, t, kb: (b, t, kb), kb=kb))
                   for kb in range(n_x)]
    else:
        n_x = 1
        x_specs = [pl.BlockSpec((1, tl, d), lambda b, t: (b, t, 0))]
    mixer_scratch = [
        pltpu.VMEM((2, n_blk, ahist + S_GROUPS, SUBLANES, ck), F32),
        pltpu.VMEM((ahist, SUBLANES, d_a), F32),
        pltpu.VMEM((n_blk, bhist + S_GROUPS, SUBLANES, ck), F32),
        pltpu.VMEM((bhist, SUBLANES, xbc_dim), F32),
        pltpu.VMEM((tl // SUBLANES, SUBLANES, d_a), F32),
        pltpu.VMEM((tl, xbc_dim), F32),
        pltpu.VMEM((tl, d_b), F32),
        pltpu.VMEM((N_STATE, d_b), F32),
        pltpu.VMEM((tl, d), BF16),
        pltpu.VMEM((n_proj, tl, ck), F32),
    ]
    out = pl.pallas_call(
        functools.partial(_even_kernel, n_x=n_x, cast_blocked=tuple(j[2] for j in cast_jobs), alpha=alpha),
        grid=(bsz, seq // tl),
        in_specs=x_specs + [_const_spec(c.shape) for c in mixer_consts] + cast_in,
        out_specs=[pl.BlockSpec((1, tl, d), lambda b, t: (b, t, 0))] + cast_out,
        out_shape=[jax.ShapeDtypeStruct(x.shape, F32)] + cast_shapes,
        scratch_shapes=mixer_scratch,
        compiler_params=pltpu.CompilerParams(
            dimension_semantics=("arbitrary", "arbitrary"), vmem_limit_bytes=VMEM_LIMIT),
        name="conformer_ssd_mixer",
    )(*([x] * n_x), *mixer_consts, *[j[0] for j in cast_jobs])
    return out[0], tuple(out[1:])


def _row(v):
    return v.reshape(1, -1).astype(F32)


def _pad_lanes(v):
    return jnp.pad(v, ((0, 0), (0, LANES - v.shape[1])))


def kernel(x, p, e_w_in, e_conv_a_w, e_conv_a_b, e_ln_a_g, e_ln_a_b, e_conv_b_w, e_conv_b_b, e_dt_bias, e_a_log, e_d_skip, e_norm_b_g, e_w_out, o_w_in, o_conv_w, o_w_out, f_w_up, f_conv_w, f_conv_b, f_w_down, ple_w_proj, ple_w_gate, ln_g, ln_b):
    depth = f_w_up.shape[0]
    alpha = (2.0 * depth) ** 0.25
    d_b = e_norm_b_g.shape[1]
    assert e_w_in.shape[2] % COL_CHUNK == e_dt_bias.shape[1]
    tl_even = min(256, x.shape[1])
    tl_odd = tl_ffn = min(512, x.shape[1])

    head_of_col = jnp.arange(d_b) // HEAD_DIM
    sel = (jnp.arange(LANES)[:, None] == head_of_col[None, :]).astype(BF16)
    expand = jnp.concatenate([sel, sel], axis=0)

    def mixer_jobs(i):
        if i % 2 == 0:
            return [(e_w_in, i // 2, True), (e_w_out, i // 2, False)]
        return [(o_w_in, i // 2, False), (o_w_out, i // 2, False)]

    def ffn_jobs(i):
        return [(f_w_up, i, False), (f_w_down, i, False), (ple_w_proj, i, False), (ple_w_gate, i, False)]

    weights = _cast_call(mixer_jobs(0), x.shape[0], x.shape[1] // tl_even)
    for i in range(depth):
        j = i // 2
        if i % 2 == 0:
            w_main, w_dt, w_out = weights
            mixer_consts = (
                w_main, w_dt,
                e_conv_a_w[j], _row(e_conv_a_b[j]), _row(e_ln_a_g[j]), _row(e_ln_a_b[j]),
                e_conv_b_w[j], _row(e_conv_b_b[j]),
                _pad_lanes(_row(e_dt_bias[j])), _pad_lanes(_row(e_a_log[j])),
                _row(jnp.repeat(e_d_skip[j], HEAD_DIM)), _row(e_norm_b_g[j]), expand,
                w_out, _row(ln_g[i, 0]), _row(ln_b[i, 0]))
            x, weights = _even_call(x, mixer_consts, ffn_jobs(i), alpha=alpha, tl=tl_even, natural_in=(i == 0))
        else:
            w_in, w_out = weights
            x, weights = _odd_call(x, w_in, o_conv_w[j], w_out, _row(ln_g[i, 0]), _row(ln_b[i, 0]),
                                   ffn_jobs(i), alpha=alpha, tl=tl_odd)
        w_up, w_down, w_proj, w_gate = weights
        ffn_consts = (w_up, f_conv_w[i], _row(f_conv_b[i]), w_down, w_proj, w_gate,
                      _row(ln_g[i, 1]), _row(ln_b[i, 1]))
        x, weights = _ffn_call(x, p, i, ffn_consts, mixer_jobs(i + 1) if i + 1 < depth else [],
                               alpha=alpha, tl=tl_ffn, natural_out=(i == depth - 1))
    return x
```

```python
import functools

import jax
import jax.numpy as jnp
from jax import lax
from jax.experimental import pallas as pl
from jax.experimental.pallas import tpu as pltpu

F32 = jnp.float32
BF16 = jnp.bfloat16

LN_EPS = 1e-5
LOG2E = 1.4426950408889634
HEAD_DIM = 64
N_GROUPS = 4
N_STATE = 128
LANES = 128
SUBLANES = 8
PERM_TILE = 128
S_GROUPS = PERM_TILE // SUBLANES
CONV_ROW_BLOCK = 16
PROJ_PER_CHUNK = 4
COL_CHUNK = 256
VMEM_LIMIT = 60 * 1024 * 1024


def _dot(a, b):
    return jnp.dot(a, b, preferred_element_type=F32)


def _layer_norm(v, g, b):
    mu = jnp.mean(v, axis=-1, keepdims=True)
    d = v - mu
    var = jnp.mean(d * d, axis=-1, keepdims=True)
    return d * lax.rsqrt(var + LN_EPS) * g + b


def _sigmoid(v):
    return 1.0 / (1.0 + jnp.exp(-v))


def _silu(v):
    return v * _sigmoid(v)


def _softplus(v):
    return jnp.maximum(v, 0.0) + jnp.log(1.0 + jnp.exp(-jnp.abs(v)))


def _load_sublane_major(refs, lead, n_blocks):
    groups = []
    for u in range(n_blocks):
        for i in range(S_GROUPS):
            rows = pl.ds(u * PERM_TILE + i, SUBLANES, stride=S_GROUPS)
            groups.append(jnp.concatenate([r[lead + (rows, slice(None))] for r in refs], axis=1))
    return jnp.concatenate(groups, axis=0)


def _stage_history(h3, stage_ref, carry, hist):
    n1 = min(hist, S_GROUPS)
    n2 = hist - n1
    assert n2 <= S_GROUPS
    rolled = pltpu.roll(h3[S_GROUPS - n1:], 1, axis=1)
    if n2:
        rolled = jnp.concatenate([pltpu.roll(h3[S_GROUPS - n2:], 2, axis=1), rolled], axis=0)
    sub = lax.broadcasted_iota(jnp.int32, rolled.shape, 1)
    grp = lax.broadcasted_iota(jnp.int32, rolled.shape, 0)
    from_prev = sub < jnp.where(grp < n2, 2, 1)
    stage_ref[0:hist] = jnp.where(from_prev, carry, rolled)
    stage_ref[hist:hist + S_GROUPS] = h3
    return rolled


def _short_conv(h, stage_ref, carry_ref, cols, w):
    tl, n = h.shape
    k = w.shape[0]
    hist = k - 1
    h3 = h.reshape(tl // SUBLANES, SUBLANES, n)
    carry = carry_ref[:, :, cols]
    outs = []
    for u in range(tl // PERM_TILE):
        hu = h3[u * S_GROUPS:(u + 1) * S_GROUPS]
        carry = _stage_history(hu, stage_ref.at[u], carry, hist)
        out = hu * w[k - 1:k, :]
        for j in range(k - 1):
            out = out + stage_ref[u, j:j + S_GROUPS] * w[j:j + 1, :]
        outs.append(out)
    carry_ref[:, :, cols] = carry
    return jnp.concatenate(outs, axis=0).reshape(tl, n)


def _const_spec(shape):
    nd = len(shape)
    return pl.BlockSpec(shape, lambda b, t: (0,) * nd, pipeline_mode=pl.Buffered(1))


BF16_SUBLANES = 16


def _cast_plan(jobs, n_b, n_t):
    steps = n_b * n_t
    in_specs, out_specs, out_shapes = [], [], []
    for w, layer, blocked in jobs:
        _, r, c = w.shape
        slabs = max(k for k in range(1, steps + 1)
                    if steps % k == 0 and r % k == 0 and (r // k) % BF16_SUBLANES == 0)
        per = steps // slabs
        rows = r // slabs
        in_specs.append(pl.BlockSpec(
            (1, rows, c), functools.partial(lambda b, t, layer, per: (layer, (b * n_t + t) // per, 0),
                                            layer=layer, per=per)))
        row_slab = functools.partial(lambda b, t, per: ((b * n_t + t) // per, 0), per=per)
        if not blocked:
            out_specs.append(pl.BlockSpec((rows, c), row_slab))
            out_shapes.append(jax.ShapeDtypeStruct((r, c), BF16))
        else:
            nb = c // COL_CHUNK
            assert 0 < c - nb * COL_CHUNK <= LANES
            out_specs.append(pl.BlockSpec(
                (nb, rows, COL_CHUNK), functools.partial(lambda b, t, per: (0, (b * n_t + t) // per, 0), per=per)))
            out_shapes.append(jax.ShapeDtypeStruct((nb, r, COL_CHUNK), BF16))
            out_specs.append(pl.BlockSpec((rows, LANES), row_slab))
            out_shapes.append(jax.ShapeDtypeStruct((r, LANES), BF16))
    return in_specs, out_specs, out_shapes


def _blocked_bf16(w):
    r, c = w.shape
    nb = c // COL_CHUNK
    wb = w.astype(BF16)
    main = wb[:, :nb * COL_CHUNK].reshape(r, nb, COL_CHUNK).transpose(1, 0, 2)
    return main, jnp.pad(wb[:, nb * COL_CHUNK:], ((0, 0), (0, LANES - (c - nb * COL_CHUNK))))


def _split_refs(refs, n_in, blocked):
    n_src = len(blocked)
    n_dst = sum(2 if b else 1 for b in blocked)
    k = n_in + n_src
    return refs[:n_in], refs[n_in:k], refs[k], refs[k + 1:k + 1 + n_dst], refs[k + 1 + n_dst:]


def _run_casts(blocked_flags, src_refs, dst_refs):
    dst = list(dst_refs)
    for blocked, src in zip(blocked_flags, src_refs):
        w = src[0].astype(BF16)
        if not blocked:
            dst.pop(0)[...] = w
        else:
            main, rest = dst.pop(0), dst.pop(0)
            nb = main.shape[0]
            for blk in range(nb):
                main[blk] = w[:, blk * COL_CHUNK:(blk + 1) * COL_CHUNK]
            tail = w[:, nb * COL_CHUNK:]
            rest[...] = jnp.concatenate(
                [tail, jnp.zeros((tail.shape[0], LANES - tail.shape[1]), BF16)], axis=1)


def _ffn_body(x, p_refs, wup_ref, cw_ref, cb_ref, wdn_ref, wproj_ref, wgate_ref, g_ref, b_ref,
              stage_ref, carry_ref, act_ref, *, alpha):
    tl = x.shape[0]
    d_ff = wdn_ref.shape[0]
    ck = COL_CHUNK
    xb = x.astype(BF16)

    def conv_cols(col0, slot):
        cols = slice(col0, col0 + ck)
        h = _dot(xb, wup_ref[:, cols])
        return _short_conv(h, stage_ref.at[slot], carry_ref, cols, cw_ref[:, cols]) + cb_ref[:, cols]

    for c in range(d_ff // ck):
        ha = conv_cols(c * ck, 0)
        hg = conv_cols(d_ff + c * ck, 1)
        act_ref[:, c * ck:(c + 1) * ck] = (_silu(ha) * hg).astype(BF16)

    ffn = _dot(act_ref[...], wdn_ref[...])
    gate = _sigmoid(_dot(xb, wgate_ref[...]))
    pe = _load_sublane_major(p_refs, (0, 0), tl // PERM_TILE)
    ple = _dot(pe.astype(BF16), wproj_ref[...]) * gate
    return _layer_norm(alpha * x + ffn + ple, g_ref[...], b_ref[...])


def _ffn_scratch(tl, d_ff, hist):
    return [
        pltpu.VMEM((2, tl // PERM_TILE, hist + S_GROUPS, SUBLANES, COL_CHUNK), F32),
        pltpu.VMEM((hist, SUBLANES, 2 * d_ff), F32),
        pltpu.VMEM((tl, d_ff), BF16),
    ]


N_FFN_CONSTS = 8


def _ffn_kernel(*refs, cast_blocked, alpha):
    ins, cast_src, o_ref, cast_dst, scratch = _split_refs(refs, 3 + N_FFN_CONSTS, cast_blocked)
    x_ref, p0_ref, p1_ref = ins[:3]
    ffn_consts = ins[3:]
    carry_ref = scratch[1]
    unperm = scratch[3:]

    @pl.when(pl.program_id(1) == 0)
    def _():
        carry_ref[...] = jnp.zeros_like(carry_ref)

    _run_casts(cast_blocked, cast_src, cast_dst)
    x = x_ref[0]
    tl, d = x.shape
    y = _ffn_body(x, (p0_ref, p1_ref), *ffn_consts, *scratch[:3], alpha=alpha)
    if not unperm:
        o_ref[0] = y
    else:
        nat_ref, = unperm
        for kb in range(d // LANES):
            nat_ref[kb] = y[:, kb * LANES:(kb + 1) * LANES]
        for u in range(tl // PERM_TILE):
            for j in range(SUBLANES):
                for i0 in range(0, S_GROUPS, SUBLANES):
                    rows = pl.ds(u * PERM_TILE + SUBLANES * i0 + j, SUBLANES, stride=SUBLANES)
                    t0 = u * PERM_TILE + S_GROUPS * j + i0
                    for kb in range(d // LANES):
                        o_ref[0, t0:t0 + SUBLANES, kb * LANES:(kb + 1) * LANES] = nat_ref[kb, rows, :]


def _ffn_call(x, p, layer, ffn_consts, cast_jobs, *, alpha, tl, natural_out):
    bsz, seq, d = x.shape
    assert len(ffn_consts) == N_FFN_CONSTS
    d_ff = ffn_consts[3].shape[0]
    hist = ffn_consts[1].shape[0] - 1
    assert p.shape[-1] == 2 * LANES and tl % PERM_TILE == 0
    scratch = _ffn_scratch(tl, d_ff, hist)
    if natural_out:
        scratch.append(pltpu.VMEM((d // LANES, tl, LANES), F32))
    cast_in, cast_out, cast_shapes = _cast_plan(cast_jobs, bsz, seq // tl)
    out = pl.pallas_call(
        functools.partial(_ffn_kernel, cast_blocked=tuple(j[2] for j in cast_jobs), alpha=alpha),
        grid=(bsz, seq // tl),
        in_specs=[
            pl.BlockSpec((1, tl, d), lambda b, t: (b, t, 0)),
            pl.BlockSpec((1, 1, tl, LANES), lambda b, t: (layer, b, t, 0)),
            pl.BlockSpec((1, 1, tl, LANES), lambda b, t: (layer, b, t, 1)),
        ] + [_const_spec(c.shape) for c in ffn_consts] + cast_in,
        out_specs=[pl.BlockSpec((1, tl, d), lambda b, t: (b, t, 0))] + cast_out,
        out_shape=[jax.ShapeDtypeStruct(x.shape, F32)] + cast_shapes,
        scratch_shapes=scratch,
        compiler_params=pltpu.CompilerParams(
            dimension_semantics=("arbitrary", "arbitrary"), vmem_limit_bytes=VMEM_LIMIT),
        name="conv_ffn",
    )(x, p, p, *ffn_consts, *[j[0] for j in cast_jobs])
    return out[0], tuple(out[1:])


def _odd_kernel(*refs, cast_blocked, alpha):
    ins, cast_src, o_ref, cast_dst, scratch = _split_refs(refs, 6, cast_blocked)
    x_ref, win_ref, cw_ref, wout_ref, g_ref, b_ref = ins
    stage_ref, carry_ref, gated_ref = scratch

    @pl.when(pl.program_id(1) == 0)
    def _():
        carry_ref[...] = jnp.zeros_like(carry_ref)

    _run_casts(cast_blocked, cast_src, cast_dst)
    x = x_ref[0]
    xb = x.astype(BF16)
    d_c = wout_ref.shape[0]
    ck = COL_CHUNK
    for c in range(d_c // ck):
        cols = slice(c * ck, (c + 1) * ck)
        bg = _dot(xb, win_ref[:, c * ck:(c + 1) * ck])
        cg = _dot(xb, win_ref[:, d_c + c * ck:d_c + (c + 1) * ck])
        v = _dot(xb, win_ref[:, 2 * d_c + c * ck:2 * d_c + (c + 1) * ck])
        conv = _short_conv(cg * v, stage_ref, carry_ref, cols, cw_ref[:, cols])
        gated_ref[:, cols] = (bg * conv).astype(BF16)
    mix = _dot(gated_ref[...], wout_ref[...])
    o_ref[0] = _layer_norm(alpha * x + mix, g_ref[...], b_ref[...])


def _odd_call(x, w_in, conv_w, w_out, ln_g, ln_b, cast_jobs, *, alpha, tl):
    bsz, seq, d = x.shape
    d_c = w_out.shape[0]
    assert tl % PERM_TILE == 0
    hist = conv_w.shape[0] - 1
    cast_in, cast_out, cast_shapes = _cast_plan(cast_jobs, bsz, seq // tl)
    out = pl.pallas_call(
        functools.partial(_odd_kernel, cast_blocked=tuple(j[2] for j in cast_jobs), alpha=alpha),
        grid=(bsz, seq // tl),
        in_specs=[
            pl.BlockSpec((1, tl, d), lambda b, t: (b, t, 0)),
            _const_spec(w_in.shape), _const_spec(conv_w.shape), _const_spec(w_out.shape),
            _const_spec(ln_g.shape), _const_spec(ln_b.shape),
        ] + cast_in,
        out_specs=[pl.BlockSpec((1, tl, d), lambda b, t: (b, t, 0))] + cast_out,
        out_shape=[jax.ShapeDtypeStruct(x.shape, F32)] + cast_shapes,
        scratch_shapes=[
            pltpu.VMEM((tl // PERM_TILE, hist + S_GROUPS, SUBLANES, COL_CHUNK), F32),
            pltpu.VMEM((hist, SUBLANES, d_c), F32),
            pltpu.VMEM((tl, d_c), BF16),
        ],
        compiler_params=pltpu.CompilerParams(
            dimension_semantics=("arbitrary", "arbitrary"), vmem_limit_bytes=VMEM_LIMIT),
        name="short_conv_mixer",
    )(x, w_in, conv_w, w_out, ln_g, ln_b, *[j[0] for j in cast_jobs])
    return out[0], tuple(out[1:])


def _proj_schedule(n_chunks, n_blocks, slots_per_chunk):
    rest = list(range(3 * n_chunks, n_blocks)) + list(range(2 * n_chunks, 3 * n_chunks))
    order = [0, n_chunks]
    for c in range(n_chunks):
        step = [c + 1, n_chunks + c + 1] if c + 1 < n_chunks else []
        while len(step) < slots_per_chunk and rest:
            step.append(rest.pop(0))
        assert len(step) == slots_per_chunk
        order += step
    return order + rest


def _split3(v):
    hi = v.astype(BF16)
    r1 = v - hi.astype(F32)
    mid = r1.astype(BF16)
    lo = (r1 - mid.astype(F32)).astype(BF16)
    return hi, mid, lo


def _mixer_body(x, win_ref, wdt_ref, caw_ref, cab_ref, lag_ref, lab_ref, cbw_ref, cbb_ref,
                dtb_ref, alog_ref, dskip_ref, nbg_ref, expand_ref, wout_ref, g_ref, b_ref,
                astage_ref, acarry_ref, bstage_ref, bcarry_ref, ca_ref, xbc_ref, ssd_ref,
                state_ref, xb_ref, proj_ref, *, alpha):
    tl = x.shape[0]
    n_blk = tl // PERM_TILE
    d_a = caw_ref.shape[1]
    d_b = nbg_ref.shape[1]
    ck = COL_CHUNK
    ka = caw_ref.shape[0]
    ahist = ka - 1
    gn = N_GROUPS * N_STATE
    n_chunks = d_a // ck
    z_blk = 2 * n_chunks
    xbc_blk = z_blk + d_b // ck
    lane_blocks = ck // LANES
    row_blocks = S_GROUPS // CONV_ROW_BLOCK
    conv_steps = n_blk * row_blocks

    xb_ref[...] = x.astype(BF16)
    n_proj = proj_ref.shape[0]
    order = _proj_schedule(n_chunks, n_proj, PROJ_PER_CHUNK)
    n_ahead = 2 + n_chunks * PROJ_PER_CHUNK

    def projection(blk):
        return _dot(xb_ref[...], win_ref[blk])

    def project(pos):
        proj_ref[order[pos]] = projection(order[pos])

    project(0)
    project(1)

    def stage_chunk(c):
        cols = slice(c * ck, (c + 1) * ck)
        a3 = (proj_ref[c] * _sigmoid(proj_ref[n_chunks + c])).reshape(tl // SUBLANES, SUBLANES, ck)
        stage = astage_ref.at[c % astage_ref.shape[0]]
        carry = acarry_ref[:, :, cols]
        for u in range(n_blk):
            carry = _stage_history(a3[u * S_GROUPS:(u + 1) * S_GROUPS], stage.at[u], carry, ahist)
        acarry_ref[:, :, cols] = carry

    def conv_chunk(c):
        stage = astage_ref.at[c % astage_ref.shape[0]]
        for it in range(conv_steps):
            u = it // row_blocks
            r0 = (it % row_blocks) * CONV_ROW_BLOCK
            for kb in range(lane_blocks):
                ls = slice(kb * LANES, (kb + 1) * LANES)
                gl = slice(c * ck + kb * LANES, c * ck + (kb + 1) * LANES)
                acc = stage[u, ahist + r0:ahist + r0 + CONV_ROW_BLOCK, :, ls] * caw_ref[ka - 1:ka, gl] + cab_ref[:, gl]
                for j in range(ka - 1):
                    acc = acc + stage[u, j + r0:j + r0 + CONV_ROW_BLOCK, :, ls] * caw_ref[j:j + 1, gl]
                g0 = u * S_GROUPS + r0
                ca_ref[g0:g0 + CONV_ROW_BLOCK, :, gl] = acc
        for k in range(PROJ_PER_CHUNK):
            project(2 + c * PROJ_PER_CHUNK + k)

    for c in range(n_chunks):
        stage_chunk(c)
        pl.when(pl.program_id(0) >= 0)(functools.partial(conv_chunk, c))
    late = {order[pos]: projection(order[pos]) for pos in range(n_ahead, n_proj)}
    ya = _silu(_layer_norm(ca_ref[...].reshape(tl, d_a), lag_ref[...], lab_ref[...]))
    mix_a = _dot(ya.astype(BF16), wout_ref[0:d_a, :])

    for c in range((d_b + 2 * gn) // ck):
        cols = slice(c * ck, (c + 1) * ck)
        h = proj_ref[xbc_blk + c]
        conv = _short_conv(h, bstage_ref, bcarry_ref, cols, cbw_ref[:, cols]) + cbb_ref[:, cols]
        xbc_ref[:, cols] = _silu(conv)

    dt_all = _softplus(_dot(xb_ref[...], wdt_ref[...]) + dtb_ref[...])
    da_all = dt_all * (-LOG2E * jnp.exp(alog_ref[...]))

    def token_of(r):
        return S_GROUPS * (r % SUBLANES) + r // SUBLANES

    q = PERM_TILE
    causal = (token_of(lax.broadcasted_iota(jnp.int32, (q, q), 0))
              >= token_of(lax.broadcasted_iota(jnp.int32, (q, q), 1)))
    tril = jnp.where(causal, 1.0, 0.0).astype(BF16)
    lane = lax.broadcasted_iota(jnp.int32, (q, LANES), 1)
    half = [lane < HEAD_DIM, lane >= HEAD_DIM]
    pairs_per_group = (d_b // LANES) // N_GROUPS

    for u in range(n_blk):
        rows = slice(u * q, (u + 1) * q)
        dt = dt_all[rows]
        hi, mid, lo = _split3(da_all[rows])
        cum = _dot(tril, hi) + _dot(tril, mid) + _dot(tril, lo)
        src_t = (cum - jnp.log(dt) * LOG2E).T
        last = cum[q - 1:q, :]
        dstate = jnp.exp2(last - cum) * dt
        pad = jnp.zeros((SUBLANES - 1, LANES), F32)
        scal = jnp.concatenate([dstate, jnp.exp2(last), pad], axis=0)
        s_hi = scal.astype(BF16)
        s_lo = (scal - s_hi.astype(F32)).astype(BF16)
        expd = _dot(jnp.concatenate([s_hi, s_lo], axis=1), expand_ref[...])
        dstate_x = expd[0:q]
        cdecay_x = expd[q:q + 1]
        st_b = state_ref[...].astype(BF16)

        for g in range(N_GROUPS):
            bm = xbc_ref[rows, d_b + g * N_STATE:d_b + (g + 1) * N_STATE].astype(BF16)
            cm = xbc_ref[rows, d_b + gn + g * N_STATE:d_b + gn + (g + 1) * N_STATE]
            cb = lax.dot_general(cm.astype(BF16), bm, (((1,), (1,)), ((), ())),
                                 preferred_element_type=F32)
            scaled = []
            for pr in range(g * pairs_per_group, (g + 1) * pairs_per_group):
                lanes = slice(pr * LANES, (pr + 1) * LANES)
                xs = xbc_ref[rows, lanes]
                xp = xs.astype(BF16)
                sp = st_b[:, lanes]
                lhs, rhs = [], []
                for j in range(2):
                    hd = 2 * pr + j
                    ccol = jnp.broadcast_to(cum[:, hd:hd + 1], (q, q))
                    srow = jnp.broadcast_to(src_t[hd:hd + 1, :], (q, q))
                    lhs.append((cb * jnp.where(causal, jnp.exp2(ccol - srow), 0.0)).astype(BF16))
                    lhs.append((cm * jnp.exp2(ccol)).astype(BF16))
                    rhs.append(jnp.where(half[j], xp, jnp.zeros_like(xp)))
                    rhs.append(jnp.where(half[j], sp, jnp.zeros_like(sp)))
                y = _dot(jnp.concatenate(lhs, axis=1), jnp.concatenate(rhs, axis=0))
                ssd_ref[rows, lanes] = y + dskip_ref[:, lanes] * xs
                scaled.append((xs * dstate_x[:, lanes]).astype(BF16))
            gcols = slice(g * pairs_per_group * LANES, (g + 1) * pairs_per_group * LANES)
            upd = lax.dot_general(bm, jnp.concatenate(scaled, axis=1), (((0,), (0,)), ((), ())),
                                  preferred_element_type=F32)
            state_ref[:, gcols] = state_ref[:, gcols] * cdecay_x[:, gcols] + upd

    z = jnp.concatenate([late[blk] if blk in late else proj_ref[blk]
                         for blk in range(z_blk, z_blk + d_b // ck)], axis=1)
    y = ssd_ref[...] * _silu(z)
    yb = y * lax.rsqrt(jnp.mean(y * y, axis=-1, keepdims=True) + LN_EPS) * nbg_ref[...]
    mix = mix_a + _dot(yb.astype(BF16), wout_ref[d_a:d_a + d_b, :])
    return _layer_norm(alpha * x + mix, g_ref[...], b_ref[...])


N_MIXER_CONSTS = 16


def _even_kernel(*refs, n_x, cast_blocked, alpha):
    ins, cast_src, o_ref, cast_dst, scratch = _split_refs(refs, n_x + N_MIXER_CONSTS, cast_blocked)
    x_refs = ins[:n_x]
    mixer_consts = ins[n_x:]
    acarry_ref, bcarry_ref, state_ref = scratch[1], scratch[3], scratch[7]
    tl = o_ref.shape[1]

    @pl.when(pl.program_id(1) == 0)
    def _():
        for r in (acarry_ref, bcarry_ref, state_ref):
            r[...] = jnp.zeros_like(r)

    _run_casts(cast_blocked, cast_src, cast_dst)
    if n_x == 1:
        x = x_refs[0][0]
    else:
        x = _load_sublane_major(x_refs, (0,), tl // PERM_TILE)
    o_ref[0] = _mixer_body(x, *mixer_consts, *scratch, alpha=alpha)


def _even_call(x, mixer_consts, cast_jobs, *, alpha, tl, natural_in):
    bsz, seq, d = x.shape
    assert len(mixer_consts) == N_MIXER_CONSTS
    w_main, conv_a_w, conv_b_w, norm_b_g = mixer_consts[0], mixer_consts[2], mixer_consts[6], mixer_consts[11]
    d_a = conv_a_w.shape[1]
    d_b = norm_b_g.shape[1]
    xbc_dim = conv_b_w.shape[1]
    ck = COL_CHUNK
    ahist = conv_a_w.shape[0] - 1
    bhist = conv_b_w.shape[0] - 1
    n_blk = tl // PERM_TILE
    assert tl % PERM_TILE == 0 and PERM_TILE == N_STATE and d_a == d_b
    n_proj = w_main.shape[0]
    assert w_main.shape[1:] == (d, ck)
    cast_in, cast_out, cast_shapes = _cast_plan(cast_jobs, bsz, seq // tl)

    if natural_in:
        n_x = d // LANES
        x_specs = [pl.BlockSpec((1, tl, LANES), functools.partial(lambda b, t, kb: (b, t, kb), kb=kb))
                   for kb in range(n_x)]
    else:
        n_x = 1
        x_specs = [pl.BlockSpec((1, tl, d), lambda b, t: (b, t, 0))]
    mixer_scratch = [
        pltpu.VMEM((2, n_blk, ahist + S_GROUPS, SUBLANES, ck), F32),
        pltpu.VMEM((ahist, SUBLANES, d_a), F32),
        pltpu.VMEM((n_blk, bhist + S_GROUPS, SUBLANES, ck), F32),
        pltpu.VMEM((bhist, SUBLANES, xbc_dim), F32),
        pltpu.VMEM((tl // SUBLANES, SUBLANES, d_a), F32),
        pltpu.VMEM((tl, xbc_dim), F32),
        pltpu.VMEM((tl, d_b), F32),
        pltpu.VMEM((N_STATE, d_b), F32),
        pltpu.VMEM((tl, d), BF16),
        pltpu.VMEM((n_proj, tl, ck), F32),
    ]
    out = pl.pallas_call(
        functools.partial(_even_kernel, n_x=n_x, cast_blocked=tuple(j[2] for j in cast_jobs), alpha=alpha),
        grid=(bsz, seq // tl),
        in_specs=x_specs + [_const_spec(c.shape) for c in mixer_consts] + cast_in,
        out_specs=[pl.BlockSpec((1, tl, d), lambda b, t: (b, t, 0))] + cast_out,
        out_shape=[jax.ShapeDtypeStruct(x.shape, F32)] + cast_shapes,
        scratch_shapes=mixer_scratch,
        compiler_params=pltpu.CompilerParams(
            dimension_semantics=("arbitrary", "arbitrary"), vmem_limit_bytes=VMEM_LIMIT),
        name="conformer_ssd_mixer",
    )(*([x] * n_x), *mixer_consts, *[j[0] for j in cast_jobs])
    return out[0], tuple(out[1:])


def _row(v):
    return v.reshape(1, -1).astype(F32)


def _pad_lanes(v):
    return jnp.pad(v, ((0, 0), (0, LANES - v.shape[1])))


def kernel(x, p, e_w_in, e_conv_a_w, e_conv_a_b, e_ln_a_g, e_ln_a_b, e_conv_b_w, e_conv_b_b, e_dt_bias, e_a_log, e_d_skip, e_norm_b_g, e_w_out, o_w_in, o_conv_w, o_w_out, f_w_up, f_conv_w, f_conv_b, f_w_down, ple_w_proj, ple_w_gate, ln_g, ln_b):
    depth = f_w_up.shape[0]
    alpha = (2.0 * depth) ** 0.25
    d_b = e_norm_b_g.shape[1]
    assert e_w_in.shape[2] % COL_CHUNK == e_dt_bias.shape[1]
    tl_even = tl_odd = min(512, x.shape[1])
    tl_ffn = min(1024, x.shape[1])

    head_of_col = jnp.arange(d_b) // HEAD_DIM
    sel = (jnp.arange(LANES)[:, None] == head_of_col[None, :]).astype(BF16)
    expand = jnp.concatenate([sel, sel], axis=0)

    def mixer_jobs(i):
        if i % 2 == 0:
            return [(e_w_in, i // 2, True), (e_w_out, i // 2, False)]
        return [(o_w_in, i // 2, False), (o_w_out, i // 2, False)]

    def ffn_jobs(i):
        return [(f_w_up, i, False), (f_w_down, i, False), (ple_w_proj, i, False), (ple_w_gate, i, False)]

    weights = _blocked_bf16(e_w_in[0]) + (e_w_out[0].astype(BF16),)
    for i in range(depth):
        j = i // 2
        if i % 2 == 0:
            w_main, w_dt, w_out = weights
            mixer_consts = (
                w_main, w_dt,
                e_conv_a_w[j], _row(e_conv_a_b[j]), _row(e_ln_a_g[j]), _row(e_ln_a_b[j]),
                e_conv_b_w[j], _row(e_conv_b_b[j]),
                _pad_lanes(_row(e_dt_bias[j])), _pad_lanes(_row(e_a_log[j])),
                _row(jnp.repeat(e_d_skip[j], HEAD_DIM)), _row(e_norm_b_g[j]), expand,
                w_out, _row(ln_g[i, 0]), _row(ln_b[i, 0]))
            x, weights = _even_call(x, mixer_consts, ffn_jobs(i), alpha=alpha, tl=tl_even, natural_in=(i == 0))
        else:
            w_in, w_out = weights
            x, weights = _odd_call(x, w_in, o_conv_w[j], w_out, _row(ln_g[i, 0]), _row(ln_b[i, 0]),
                                   ffn_jobs(i), alpha=alpha, tl=tl_odd)
        w_up, w_down, w_proj, w_gate = weights
        ffn_consts = (w_up, f_conv_w[i], _row(f_conv_b[i]), w_down, w_proj, w_gate,
                      _row(ln_g[i, 1]), _row(ln_b[i, 1]))
        x, weights = _ffn_call(x, p, i, ffn_consts, mixer_jobs(i + 1) if i + 1 < depth else [],
                               alpha=alpha, tl=tl_ffn, natural_out=(i == depth - 1))
    return x
```

```python
import functools

import jax
import jax.numpy as jnp
from jax import lax
from jax.experimental import pallas as pl
from jax.experimental.pallas import tpu as pltpu

F32 = jnp.float32
BF16 = jnp.bfloat16

LN_EPS = 1e-5
LOG2E = 1.4426950408889634
HEAD_DIM = 64
N_GROUPS = 4
N_STATE = 128
LANES = 128
SUBLANES = 8
PERM_TILE = 128
S_GROUPS = PERM_TILE // SUBLANES
CONV_ROW_BLOCK = 16
PROJ_PER_CHUNK = 4
COL_CHUNK = 256
VMEM_LIMIT = 60 * 1024 * 1024


def _dot(a, b):
    return jnp.dot(a, b, preferred_element_type=F32)


def _layer_norm(v, g, b):
    mu = jnp.mean(v, axis=-1, keepdims=True)
    d = v - mu
    var = jnp.mean(d * d, axis=-1, keepdims=True)
    return d * lax.rsqrt(var + LN_EPS) * g + b


def _sigmoid(v):
    return 1.0 / (1.0 + jnp.exp(-v))


def _silu(v):
    return v * _sigmoid(v)


def _softplus(v):
    return jnp.maximum(v, 0.0) + jnp.log(1.0 + jnp.exp(-jnp.abs(v)))


def _load_sublane_major(refs, lead, n_blocks):
    groups = []
    for u in range(n_blocks):
        for i in range(S_GROUPS):
            rows = pl.ds(u * PERM_TILE + i, SUBLANES, stride=S_GROUPS)
            groups.append(jnp.concatenate([r[lead + (rows, slice(None))] for r in refs], axis=1))
    return jnp.concatenate(groups, axis=0)


def _stage_history(h3, stage_ref, carry, hist):
    n1 = min(hist, S_GROUPS)
    n2 = hist - n1
    assert n2 <= S_GROUPS
    rolled = pltpu.roll(h3[S_GROUPS - n1:], 1, axis=1)
    if n2:
        rolled = jnp.concatenate([pltpu.roll(h3[S_GROUPS - n2:], 2, axis=1), rolled], axis=0)
    sub = lax.broadcasted_iota(jnp.int32, rolled.shape, 1)
    grp = lax.broadcasted_iota(jnp.int32, rolled.shape, 0)
    from_prev = sub < jnp.where(grp < n2, 2, 1)
    stage_ref[0:hist] = jnp.where(from_prev, carry, rolled)
    stage_ref[hist:hist + S_GROUPS] = h3
    return rolled


def _short_conv(h, stage_ref, carry_ref, cols, w):
    tl, n = h.shape
    k = w.shape[0]
    hist = k - 1
    h3 = h.reshape(tl // SUBLANES, SUBLANES, n)
    carry = carry_ref[:, :, cols]
    outs = []
    for u in range(tl // PERM_TILE):
        hu = h3[u * S_GROUPS:(u + 1) * S_GROUPS]
        carry = _stage_history(hu, stage_ref.at[u], carry, hist)
        out = hu * w[k - 1:k, :]
        for j in range(k - 1):
            out = out + stage_ref[u, j:j + S_GROUPS] * w[j:j + 1, :]
        outs.append(out)
    carry_ref[:, :, cols] = carry
    return jnp.concatenate(outs, axis=0).reshape(tl, n)


def _const_spec(shape):
    nd = len(shape)
    return pl.BlockSpec(shape, lambda b, t: (0,) * nd, pipeline_mode=pl.Buffered(1))


BF16_SUBLANES = 16


def _cast_plan(jobs, n_b, n_t):
    steps = n_b * n_t
    in_specs, out_specs, out_shapes = [], [], []
    for w, layer, blocked in jobs:
        _, r, c = w.shape
        slabs = max(k for k in range(1, steps + 1)
                    if steps % k == 0 and r % k == 0 and (r // k) % BF16_SUBLANES == 0)
        per = steps // slabs
        rows = r // slabs
        in_specs.append(pl.BlockSpec(
            (1, rows, c), functools.partial(lambda b, t, layer, per: (layer, (b * n_t + t) // per, 0),
                                            layer=layer, per=per)))
        row_slab = functools.partial(lambda b, t, per: ((b * n_t + t) // per, 0), per=per)
        if not blocked:
            out_specs.append(pl.BlockSpec((rows, c), row_slab))
            out_shapes.append(jax.ShapeDtypeStruct((r, c), BF16))
        else:
            nb = c // COL_CHUNK
            assert 0 < c - nb * COL_CHUNK <= LANES
            out_specs.append(pl.BlockSpec(
                (nb, rows, COL_CHUNK), functools.partial(lambda b, t, per: (0, (b * n_t + t) // per, 0), per=per)))
            out_shapes.append(jax.ShapeDtypeStruct((nb, r, COL_CHUNK), BF16))
            out_specs.append(pl.BlockSpec((rows, LANES), row_slab))
            out_shapes.append(jax.ShapeDtypeStruct((r, LANES), BF16))
    return in_specs, out_specs, out_shapes


def _blocked_bf16(w):
    r, c = w.shape
    nb = c // COL_CHUNK
    wb = w.astype(BF16)
    main = wb[:, :nb * COL_CHUNK].reshape(r, nb, COL_CHUNK).transpose(1, 0, 2)
    return main, jnp.pad(wb[:, nb * COL_CHUNK:], ((0, 0), (0, LANES - (c - nb * COL_CHUNK))))


def _split_refs(refs, n_in, blocked):
    n_src = len(blocked)
    n_dst = sum(2 if b else 1 for b in blocked)
    k = n_in + n_src
    return refs[:n_in], refs[n_in:k], refs[k], refs[k + 1:k + 1 + n_dst], refs[k + 1 + n_dst:]


def _run_casts(blocked_flags, src_refs, dst_refs):
    dst = list(dst_refs)
    for blocked, src in zip(blocked_flags, src_refs):
        w = src[0].astype(BF16)
        if not blocked:
            dst.pop(0)[...] = w
        else:
            main, rest = dst.pop(0), dst.pop(0)
            nb = main.shape[0]
            for blk in range(nb):
                main[blk] = w[:, blk * COL_CHUNK:(blk + 1) * COL_CHUNK]
            tail = w[:, nb * COL_CHUNK:]
            rest[...] = jnp.concatenate(
                [tail, jnp.zeros((tail.shape[0], LANES - tail.shape[1]), BF16)], axis=1)


def _ffn_body(x, p_refs, wup_ref, cw_ref, cb_ref, wdn_ref, wproj_ref, wgate_ref, g_ref, b_ref,
              stage_ref, carry_ref, act_ref, *, alpha):
    tl = x.shape[0]
    d_ff = wdn_ref.shape[0]
    ck = COL_CHUNK
    xb = x.astype(BF16)

    def conv_cols(col0, slot):
        cols = slice(col0, col0 + ck)
        h = _dot(xb, wup_ref[:, cols])
        return _short_conv(h, stage_ref.at[slot], carry_ref, cols, cw_ref[:, cols]) + cb_ref[:, cols]

    for c in range(d_ff // ck):
        ha = conv_cols(c * ck, 0)
        hg = conv_cols(d_ff + c * ck, 1)
        act_ref[:, c * ck:(c + 1) * ck] = (_silu(ha) * hg).astype(BF16)

    ffn = _dot(act_ref[...], wdn_ref[...])
    gate = _sigmoid(_dot(xb, wgate_ref[...]))
    pe = _load_sublane_major(p_refs, (0, 0), tl // PERM_TILE)
    ple = _dot(pe.astype(BF16), wproj_ref[...]) * gate
    return _layer_norm(alpha * x + ffn + ple, g_ref[...], b_ref[...])


def _ffn_scratch(tl, d_ff, hist):
    return [
        pltpu.VMEM((2, tl // PERM_TILE, hist + S_GROUPS, SUBLANES, COL_CHUNK), F32),
        pltpu.VMEM((hist, SUBLANES, 2 * d_ff), F32),
        pltpu.VMEM((tl, d_ff), BF16),
    ]


N_FFN_CONSTS = 8


def _ffn_kernel(*refs, cast_blocked, alpha):
    ins, cast_src, o_ref, cast_dst, scratch = _split_refs(refs, 3 + N_FFN_CONSTS, cast_blocked)
    x_ref, p0_ref, p1_ref = ins[:3]
    ffn_consts = ins[3:]
    carry_ref = scratch[1]
    unperm = scratch[3:]

    @pl.when(pl.program_id(1) == 0)
    def _():
        carry_ref[...] = jnp.zeros_like(carry_ref)

    _run_casts(cast_blocked, cast_src, cast_dst)
    x = x_ref[0]
    tl, d = x.shape
    y = _ffn_body(x, (p0_ref, p1_ref), *ffn_consts, *scratch[:3], alpha=alpha)
    if not unperm:
        o_ref[0] = y
    else:
        nat_ref, = unperm
        for kb in range(d // LANES):
            nat_ref[kb] = y[:, kb * LANES:(kb + 1) * LANES]
        for u in range(tl // PERM_TILE):
            for j in range(SUBLANES):
                for i0 in range(0, S_GROUPS, SUBLANES):
                    rows = pl.ds(u * PERM_TILE + SUBLANES * i0 + j, SUBLANES, stride=SUBLANES)
                    t0 = u * PERM_TILE + S_GROUPS * j + i0
                    for kb in range(d // LANES):
                        o_ref[0, t0:t0 + SUBLANES, kb * LANES:(kb + 1) * LANES] = nat_ref[kb, rows, :]


def _ffn_call(x, p, layer, ffn_consts, cast_jobs, *, alpha, tl, natural_out):
    bsz, seq, d = x.shape
    assert len(ffn_consts) == N_FFN_CONSTS
    d_ff = ffn_consts[3].shape[0]
    hist = ffn_consts[1].shape[0] - 1
    assert p.shape[-1] == 2 * LANES and tl % PERM_TILE == 0
    scratch = _ffn_scratch(tl, d_ff, hist)
    if natural_out:
        scratch.append(pltpu.VMEM((d // LANES, tl, LANES), F32))
    cast_in, cast_out, cast_shapes = _cast_plan(cast_jobs, bsz, seq // tl)
    out = pl.pallas_call(
        functools.partial(_ffn_kernel, cast_blocked=tuple(j[2] for j in cast_jobs), alpha=alpha),
        grid=(bsz, seq // tl),
        in_specs=[
            pl.BlockSpec((1, tl, d), lambda b, t: (b, t, 0)),
            pl.BlockSpec((1, 1, tl, LANES), lambda b, t: (layer, b, t, 0)),
            pl.BlockSpec((1, 1, tl, LANES), lambda b, t: (layer, b, t, 1)),
        ] + [_const_spec(c.shape) for c in ffn_consts] + cast_in,
        out_specs=[pl.BlockSpec((1, tl, d), lambda b, t: (b, t, 0))] + cast_out,
        out_shape=[jax.ShapeDtypeStruct(x.shape, F32)] + cast_shapes,
        scratch_shapes=scratch,
        compiler_params=pltpu.CompilerParams(
            dimension_semantics=("arbitrary", "arbitrary"), vmem_limit_bytes=VMEM_LIMIT),
        name="conv_ffn",
    )(x, p, p, *ffn_consts, *[j[0] for j in cast_jobs])
    return out[0], tuple(out[1:])


def _odd_kernel(*refs, cast_blocked, alpha):
    ins, cast_src, o_ref, cast_dst, scratch = _split_refs(refs, 6, cast_blocked)
    x_ref, win_ref, cw_ref, wout_ref, g_ref, b_ref = ins
    stage_ref, carry_ref, gated_ref = scratch

    @pl.when(pl.program_id(1) == 0)
    def _():
        carry_ref[...] = jnp.zeros_like(carry_ref)

    _run_casts(cast_blocked, cast_src, cast_dst)
    x = x_ref[0]
    xb = x.astype(BF16)
    d_c = wout_ref.shape[0]
    ck = COL_CHUNK
    for c in range(d_c // ck):
        cols = slice(c * ck, (c + 1) * ck)
        bg = _dot(xb, win_ref[:, c * ck:(c + 1) * ck])
        cg = _dot(xb, win_ref[:, d_c + c * ck:d_c + (c + 1) * ck])
        v = _dot(xb, win_ref[:, 2 * d_c + c * ck:2 * d_c + (c + 1) * ck])
        conv = _short_conv(cg * v, stage_ref, carry_ref, cols, cw_ref[:, cols])
        gated_ref[:, cols] = (bg * conv).astype(BF16)
    mix = _dot(gated_ref[...], wout_ref[...])
    o_ref[0] = _layer_norm(alpha * x + mix, g_ref[...], b_ref[...])


def _odd_call(x, w_in, conv_w, w_out, ln_g, ln_b, cast_jobs, *, alpha, tl):
    bsz, seq, d = x.shape
    d_c = w_out.shape[0]
    assert tl % PERM_TILE == 0
    hist = conv_w.shape[0] - 1
    cast_in, cast_out, cast_shapes = _cast_plan(cast_jobs, bsz, seq // tl)
    out = pl.pallas_call(
        functools.partial(_odd_kernel, cast_blocked=tuple(j[2] for j in cast_jobs), alpha=alpha),
        grid=(bsz, seq // tl),
        in_specs=[
            pl.BlockSpec((1, tl, d), lambda b, t: (b, t, 0)),
            _const_spec(w_in.shape), _const_spec(conv_w.shape), _const_spec(w_out.shape),
            _const_spec(ln_g.shape), _const_spec(ln_b.shape),
        ] + cast_in,
        out_specs=[pl.BlockSpec((1, tl, d), lambda b, t: (b, t, 0))] + cast_out,
        out_shape=[jax.ShapeDtypeStruct(x.shape, F32)] + cast_shapes,
        scratch_shapes=[
            pltpu.VMEM((tl // PERM_TILE, hist + S_GROUPS, SUBLANES, COL_CHUNK), F32),
            pltpu.VMEM((hist, SUBLANES, d_c), F32),
            pltpu.VMEM((tl, d_c), BF16),
        ],
        compiler_params=pltpu.CompilerParams(
            dimension_semantics=("arbitrary", "arbitrary"), vmem_limit_bytes=VMEM_LIMIT),
        name="short_conv_mixer",
    )(x, w_in, conv_w, w_out, ln_g, ln_b, *[j[0] for j in cast_jobs])
    return out[0], tuple(out[1:])


def _proj_schedule(n_chunks, n_blocks, slots_per_chunk):
    rest = list(range(3 * n_chunks, n_blocks)) + list(range(2 * n_chunks, 3 * n_chunks))
    order = [0, n_chunks]
    for c in range(n_chunks):
        step = [c + 1, n_chunks + c + 1] if c + 1 < n_chunks else []
        while len(step) < slots_per_chunk and rest:
            step.append(rest.pop(0))
        assert len(step) == slots_per_chunk
        order += step
    return order + rest


def _token_cumsum(v):
    n = v.shape[1]
    v3 = v.reshape(S_GROUPS, SUBLANES, n)
    run = [v3[0]]
    for i in range(1, S_GROUPS):
        run.append(run[-1] + v3[i])
    sub = lax.broadcasted_iota(jnp.int32, (SUBLANES, n), 0)
    incl = run[-1]
    for sh in (1, 2, 4):
        incl = incl + jnp.where(sub >= sh, pltpu.roll(incl, sh, axis=0), 0.0)
    before = incl - run[-1]
    return jnp.stack([r + before for r in run], axis=0).reshape(S_GROUPS * SUBLANES, n)


def _mixer_body(x, win_ref, wdt_ref, caw_ref, cab_ref, lag_ref, lab_ref, cbw_ref, cbb_ref,
                dtb_ref, alog_ref, dskip_ref, nbg_ref, expand_ref, wout_ref, g_ref, b_ref,
                astage_ref, acarry_ref, bstage_ref, bcarry_ref, ca_ref, xbc_ref, ssd_ref,
                state_ref, xb_ref, proj_ref, *, alpha):
    tl = x.shape[0]
    n_blk = tl // PERM_TILE
    d_a = caw_ref.shape[1]
    d_b = nbg_ref.shape[1]
    ck = COL_CHUNK
    ka = caw_ref.shape[0]
    ahist = ka - 1
    gn = N_GROUPS * N_STATE
    n_chunks = d_a // ck
    z_blk = 2 * n_chunks
    xbc_blk = z_blk + d_b // ck
    lane_blocks = ck // LANES
    row_blocks = S_GROUPS // CONV_ROW_BLOCK
    conv_steps = n_blk * row_blocks

    xb_ref[...] = x.astype(BF16)
    n_proj = proj_ref.shape[0]
    order = _proj_schedule(n_chunks, n_proj, PROJ_PER_CHUNK)
    n_ahead = 2 + n_chunks * PROJ_PER_CHUNK

    def projection(blk):
        return _dot(xb_ref[...], win_ref[blk])

    def project(pos):
        proj_ref[order[pos]] = projection(order[pos])

    project(0)
    project(1)

    def stage_chunk(c):
        cols = slice(c * ck, (c + 1) * ck)
        a3 = (proj_ref[c] * _sigmoid(proj_ref[n_chunks + c])).reshape(tl // SUBLANES, SUBLANES, ck)
        stage = astage_ref.at[c % astage_ref.shape[0]]
        carry = acarry_ref[:, :, cols]
        for u in range(n_blk):
            carry = _stage_history(a3[u * S_GROUPS:(u + 1) * S_GROUPS], stage.at[u], carry, ahist)
        acarry_ref[:, :, cols] = carry

    def conv_chunk(c):
        stage = astage_ref.at[c % astage_ref.shape[0]]
        for it in range(conv_steps):
            u = it // row_blocks
            r0 = (it % row_blocks) * CONV_ROW_BLOCK
            for kb in range(lane_blocks):
                ls = slice(kb * LANES, (kb + 1) * LANES)
                gl = slice(c * ck + kb * LANES, c * ck + (kb + 1) * LANES)
                acc = stage[u, ahist + r0:ahist + r0 + CONV_ROW_BLOCK, :, ls] * caw_ref[ka - 1:ka, gl] + cab_ref[:, gl]
                for j in range(ka - 1):
                    acc = acc + stage[u, j + r0:j + r0 + CONV_ROW_BLOCK, :, ls] * caw_ref[j:j + 1, gl]
                g0 = u * S_GROUPS + r0
                ca_ref[g0:g0 + CONV_ROW_BLOCK, :, gl] = acc
        for k in range(PROJ_PER_CHUNK):
            project(2 + c * PROJ_PER_CHUNK + k)

    for c in range(n_chunks):
        stage_chunk(c)
        pl.when(pl.program_id(0) >= 0)(functools.partial(conv_chunk, c))
    late = {order[pos]: projection(order[pos]) for pos in range(n_ahead, n_proj)}
    ya = _silu(_layer_norm(ca_ref[...].reshape(tl, d_a), lag_ref[...], lab_ref[...]))
    mix_a = _dot(ya.astype(BF16), wout_ref[0:d_a, :])

    for c in range((d_b + 2 * gn) // ck):
        cols = slice(c * ck, (c + 1) * ck)
        h = proj_ref[xbc_blk + c]
        conv = _short_conv(h, bstage_ref, bcarry_ref, cols, cbw_ref[:, cols]) + cbb_ref[:, cols]
        xbc_ref[:, cols] = _silu(conv)

    dt_all = _softplus(_dot(xb_ref[...], wdt_ref[...]) + dtb_ref[...])
    da_all = dt_all * (-LOG2E * jnp.exp(alog_ref[...]))

    def token_of(r):
        return S_GROUPS * (r % SUBLANES) + r // SUBLANES

    q = PERM_TILE
    causal = (token_of(lax.broadcasted_iota(jnp.int32, (q, q), 0))
              >= token_of(lax.broadcasted_iota(jnp.int32, (q, q), 1)))
    lane = lax.broadcasted_iota(jnp.int32, (q, LANES), 1)
    half = [lane < HEAD_DIM, lane >= HEAD_DIM]
    pairs_per_group = (d_b // LANES) // N_GROUPS

    for u in range(n_blk):
        rows = slice(u * q, (u + 1) * q)
        dt = dt_all[rows]
        cum = _token_cumsum(da_all[rows])
        src_t = (cum - jnp.log(dt) * LOG2E).T
        last = cum[q - 1:q, :]
        dstate = jnp.exp2(last - cum) * dt
        pad = jnp.zeros((SUBLANES - 1, LANES), F32)
        scal = jnp.concatenate([dstate, jnp.exp2(last), pad], axis=0)
        s_hi = scal.astype(BF16)
        s_lo = (scal - s_hi.astype(F32)).astype(BF16)
        expd = _dot(jnp.concatenate([s_hi, s_lo], axis=1), expand_ref[...])
        dstate_x = expd[0:q]
        cdecay_x = expd[q:q + 1]
        st_b = state_ref[...].astype(BF16)

        for g in range(N_GROUPS):
            bm = xbc_ref[rows, d_b + g * N_STATE:d_b + (g + 1) * N_STATE].astype(BF16)
            cm = xbc_ref[rows, d_b + gn + g * N_STATE:d_b + gn + (g + 1) * N_STATE]
            cb = lax.dot_general(cm.astype(BF16), bm, (((1,), (1,)), ((), ())),
                                 preferred_element_type=F32)
            scaled = []
            for pr in range(g * pairs_per_group, (g + 1) * pairs_per_group):
                lanes = slice(pr * LANES, (pr + 1) * LANES)
                xs = xbc_ref[rows, lanes]
                xp = xs.astype(BF16)
                sp = st_b[:, lanes]
                lhs, rhs = [], []
                for j in range(2):
                    hd = 2 * pr + j
                    ccol = jnp.broadcast_to(cum[:, hd:hd + 1], (q, q))
                    srow = jnp.broadcast_to(src_t[hd:hd + 1, :], (q, q))
                    lhs.append((cb * jnp.where(causal, jnp.exp2(ccol - srow), 0.0)).astype(BF16))
                    lhs.append((cm * jnp.exp2(ccol)).astype(BF16))
                    rhs.append(jnp.where(half[j], xp, jnp.zeros_like(xp)))
                    rhs.append(jnp.where(half[j], sp, jnp.zeros_like(sp)))
                y = _dot(jnp.concatenate(lhs, axis=1), jnp.concatenate(rhs, axis=0))
                ssd_ref[rows, lanes] = y + dskip_ref[:, lanes] * xs
                scaled.append((xs * dstate_x[:, lanes]).astype(BF16))
            gcols = slice(g * pairs_per_group * LANES, (g + 1) * pairs_per_group * LANES)
            upd = lax.dot_general(bm, jnp.concatenate(scaled, axis=1), (((0,), (0,)), ((), ())),
                                  preferred_element_type=F32)
            state_ref[:, gcols] = state_ref[:, gcols] * cdecay_x[:, gcols] + upd

    z = jnp.concatenate([late[blk] if blk in late else proj_ref[blk]
                         for blk in range(z_blk, z_blk + d_b // ck)], axis=1)
    y = ssd_ref[...] * _silu(z)
    yb = y * lax.rsqrt(jnp.mean(y * y, axis=-1, keepdims=True) + LN_EPS) * nbg_ref[...]
    mix = mix_a + _dot(yb.astype(BF16), wout_ref[d_a:d_a + d_b, :])
    return _layer_norm(alpha * x + mix, g_ref[...], b_ref[...])


N_MIXER_CONSTS = 16


def _even_kernel(*refs, n_x, cast_blocked, alpha):
    ins, cast_src, o_ref, cast_dst, scratch = _split_refs(refs, n_x + N_MIXER_CONSTS, cast_blocked)
    x_refs = ins[:n_x]
    mixer_consts = ins[n_x:]
    acarry_ref, bcarry_ref, state_ref = scratch[1], scratch[3], scratch[7]
    tl = o_ref.shape[1]

    @pl.when(pl.program_id(1) == 0)
    def _():
        for r in (acarry_ref, bcarry_ref, state_ref):
            r[...] = jnp.zeros_like(r)

    _run_casts(cast_blocked, cast_src, cast_dst)
    if n_x == 1:
        x = x_refs[0][0]
    else:
        x = _load_sublane_major(x_refs, (0,), tl // PERM_TILE)
    o_ref[0] = _mixer_body(x, *mixer_consts, *scratch, alpha=alpha)


def _even_call(x, mixer_consts, cast_jobs, *, alpha, tl, natural_in):
    bsz, seq, d = x.shape
    assert len(mixer_consts) == N_MIXER_CONSTS
    w_main, conv_a_w, conv_b_w, norm_b_g = mixer_consts[0], mixer_consts[2], mixer_consts[6], mixer_consts[11]
    d_a = conv_a_w.shape[1]
    d_b = norm_b_g.shape[1]
    xbc_dim = conv_b_w.shape[1]
    ck = COL_CHUNK
    ahist = conv_a_w.shape[0] - 1
    bhist = conv_b_w.shape[0] - 1
    n_blk = tl // PERM_TILE
    assert tl % PERM_TILE == 0 and PERM_TILE == N_STATE and d_a == d_b
    n_proj = w_main.shape[0]
    assert w_main.shape[1:] == (d, ck)
    cast_in, cast_out, cast_shapes = _cast_plan(cast_jobs, bsz, seq // tl)

    if natural_in:
        n_x = d // LANES
        x_specs = [pl.BlockSpec((1, tl, LANES), functools.partial(lambda b, t, kb: (b, t, kb), kb=kb))
                   for kb in range(n_x)]
    else:
        n_x = 1
        x_specs = [pl.BlockSpec((1, tl, d), lambda b, t: (b, t, 0))]
    mixer_scratch = [
        pltpu.VMEM((2, n_blk, ahist + S_GROUPS, SUBLANES, ck), F32),
        pltpu.VMEM((ahist, SUBLANES, d_a), F32),
        pltpu.VMEM((n_blk, bhist + S_GROUPS, SUBLANES, ck), F32),
        pltpu.VMEM((bhist, SUBLANES, xbc_dim), F32),
        pltpu.VMEM((tl // SUBLANES, SUBLANES, d_a), F32),
        pltpu.VMEM((tl, xbc_dim), F32),
        pltpu.VMEM((tl, d_b), F32),
        pltpu.VMEM((N_STATE, d_b), F32),
        pltpu.VMEM((tl, d), BF16),
        pltpu.VMEM((n_proj, tl, ck), F32),
    ]
    out = pl.pallas_call(
        functools.partial(_even_kernel, n_x=n_x, cast_blocked=tuple(j[2] for j in cast_jobs), alpha=alpha),
        grid=(bsz, seq // tl),
        in_specs=x_specs + [_const_spec(c.shape) for c in mixer_consts] + cast_in,
        out_specs=[pl.BlockSpec((1, tl, d), lambda b, t: (b, t, 0))] + cast_out,
        out_shape=[jax.ShapeDtypeStruct(x.shape, F32)] + cast_shapes,
        scratch_shapes=mixer_scratch,
        compiler_params=pltpu.CompilerParams(
            dimension_semantics=("arbitrary", "arbitrary"), vmem_limit_bytes=VMEM_LIMIT),
        name="conformer_ssd_mixer",
    )(*([x] * n_x), *mixer_consts, *[j[0] for j in cast_jobs])
    return out[0], tuple(out[1:])


def _row(v):
    return v.reshape(1, -1).astype(F32)


def _pad_lanes(v):
    return jnp.pad(v, ((0, 0), (0, LANES - v.shape[1])))


def kernel(x, p, e_w_in, e_conv_a_w, e_conv_a_b, e_ln_a_g, e_ln_a_b, e_conv_b_w, e_conv_b_b, e_dt_bias, e_a_log, e_d_skip, e_norm_b_g, e_w_out, o_w_in, o_conv_w, o_w_out, f_w_up, f_conv_w, f_conv_b, f_w_down, ple_w_proj, ple_w_gate, ln_g, ln_b):
    depth = f_w_up.shape[0]
    alpha = (2.0 * depth) ** 0.25
    d_b = e_norm_b_g.shape[1]
    assert e_w_in.shape[2] % COL_CHUNK == e_dt_bias.shape[1]
    tl_even = tl_odd = min(512, x.shape[1])
    tl_ffn = min(1024, x.shape[1])

    head_of_col = jnp.arange(d_b) // HEAD_DIM
    sel = (jnp.arange(LANES)[:, None] == head_of_col[None, :]).astype(BF16)
    expand = jnp.concatenate([sel, sel], axis=0)

    def mixer_jobs(i):
        if i % 2 == 0:
            return [(e_w_in, i // 2, True), (e_w_out, i // 2, False)]
        return [(o_w_in, i // 2, False), (o_w_out, i // 2, False)]

    def ffn_jobs(i):
        return [(f_w_up, i, False), (f_w_down, i, False), (ple_w_proj, i, False), (ple_w_gate, i, False)]

    weights = _blocked_bf16(e_w_in[0]) + (e_w_out[0].astype(BF16),)
    for i in range(depth):
        j = i // 2
        if i % 2 == 0:
            w_main, w_dt, w_out = weights
            mixer_consts = (
                w_main, w_dt,
                e_conv_a_w[j], _row(e_conv_a_b[j]), _row(e_ln_a_g[j]), _row(e_ln_a_b[j]),
                e_conv_b_w[j], _row(e_conv_b_b[j]),
                _pad_lanes(_row(e_dt_bias[j])), _pad_lanes(_row(e_a_log[j])),
                _row(jnp.repeat(e_d_skip[j], HEAD_DIM)), _row(e_norm_b_g[j]), expand,
                w_out, _row(ln_g[i, 0]), _row(ln_b[i, 0]))
            x, weights = _even_call(x, mixer_consts, ffn_jobs(i), alpha=alpha, tl=tl_even, natural_in=(i == 0))
        else:
            w_in, w_out = weights
            x, weights = _odd_call(x, w_in, o_conv_w[j], w_out, _row(ln_g[i, 0]), _row(ln_b[i, 0]),
                                   ffn_jobs(i), alpha=alpha, tl=tl_odd)
        w_up, w_down, w_proj, w_gate = weights
        ffn_consts = (w_up, f_conv_w[i], _row(f_conv_b[i]), w_down, w_proj, w_gate,
                      _row(ln_g[i, 1]), _row(ln_b[i, 1]))
        x, weights = _ffn_call(x, p, i, ffn_consts, mixer_jobs(i + 1) if i + 1 < depth else [],
                               alpha=alpha, tl=tl_ffn, natural_out=(i == depth - 1))
    return x
```
